```python
import jax
import jax.numpy as jnp
from jax import lax
import numpy as np

D_MODEL = 1024
BATCH = 4
SEQ = 8192
DEPTH = 4

GRID_W = 64
CTX_LEN = 256
HEAD_DIM = 64
ROPE_BASE = 10000.0
NORM_EPS = 1e-6
NEG_INF = -1e30
Q_BLOCK = 128

A_HEADS = 8
A_KV_HEADS = 2
A_WINDOW = 128
B_HEADS = 8
B_NOPE = 64
B_ROPE = 32
B_V = 64
B_Q_RANK = 768
B_KV_RANK = 256
C_HEADS = 16
NA_ROWS = 8
NA_COLS = 16

MIX_WIDTH = A_HEADS * HEAD_DIM + B_HEADS * B_V
EVEN_SPLIT = (A_HEADS * HEAD_DIM, A_KV_HEADS * HEAD_DIM, A_KV_HEADS * HEAD_DIM, B_Q_RANK, B_KV_RANK, B_ROPE)
EVEN_IN = sum(EVEN_SPLIT)
ODD_WIDTH = C_HEADS * HEAD_DIM
ODD_IN = 3 * ODD_WIDTH

N_EXPERTS = 32
TOP_K = 4
D_EXPERT = 1024
SWIGLU_LIMIT = 7.0
SWIGLU_ALPHA = 1.702
MOE_BLOCK = 256

N_EVEN = (DEPTH + 1) // 2
N_ODD = DEPTH // 2

kernel_name = 'hybrid_dit_swa_mla_natten_moe'


def _rmsnorm(x, g):
    x32 = x.astype(jnp.float32)
    y = x32 * lax.rsqrt(jnp.mean(x32 * x32, axis=-1, keepdims=True) + NORM_EPS)
    return (y * g.astype(jnp.float32)).astype(x.dtype)


def _modulate(x, g, shift, scale):
    return _rmsnorm(x, g) * (1.0 + scale) + shift


def _split_cols(p, sizes):
    return jnp.split(p, np.cumsum(sizes)[:-1].tolist(), axis=-1)


def _rope_1d(x, pos):
    n = x.shape[-1]
    half = n // 2
    inv_freq = ROPE_BASE ** (-(jnp.arange(half, dtype=jnp.float32) / half))
    ang = pos.astype(jnp.float32)[:, None] * inv_freq
    shape = (ang.shape[0],) + (1,) * (x.ndim - 3) + (half,)
    cos = jnp.cos(ang).reshape(shape).astype(x.dtype)
    sin = jnp.sin(ang).reshape(shape).astype(x.dtype)
    x1, x2 = x[..., :half], x[..., half:]
    return jnp.concatenate([x1 * cos - x2 * sin, x2 * cos + x1 * sin], axis=-1)


def _rope2d(x, rows, cols):
    h = x.shape[-1] // 2
    return jnp.concatenate([_rope_1d(x[..., :h], rows), _rope_1d(x[..., h:], cols)], axis=-1)


def _attend_dense(q, k, v, sink):
    bsz, n, h, d = q.shape
    kv = k.shape[2]
    m = k.shape[1]
    qg = q.reshape(bsz, n, kv, h // kv, d)
    s = jnp.einsum('bqkgd,bmkd->bkgqm', qg, k).astype(jnp.float32) * (d ** -0.5)
    if sink is not None:
        sk = jnp.broadcast_to(sink.astype(jnp.float32).reshape(kv, h // kv, 1, 1), s.shape[:-1] + (1,))
        s = jnp.concatenate([s, sk], axis=-1)
    p = jax.nn.softmax(s, axis=-1).astype(q.dtype)[..., :m]
    return jnp.einsum('bkgqm,bmkd->bqkgd', p, v).reshape(bsz, n, h * d)


def _window_attn_latent(q, k, v, kc, vc, sink):
    bsz, s_len, h, d = q.shape
    kv = k.shape[2]
    g = h // kv
    w = A_WINDOW
    nb = s_len // w
    n_ctx = kc.shape[1]
    qb = jnp.swapaxes(q.reshape(bsz, nb, w, kv, g, d), 0, 1)
    pad = ((0, 0), (w, w), (0, 0), (0, 0))
    kp = jnp.pad(k, pad)
    vp = jnp.pad(v, pad)
    sink_l = sink.astype(jnp.float32).reshape(kv, g, 1, 1)
    band = jnp.abs(jnp.arange(w)[:, None] + w - jnp.arange(3 * w)[None, :]) <= w
    scale = d ** -0.5

    def block(args):
        n, qn = args
        kn = lax.dynamic_slice_in_dim(kp, n * w, 3 * w, axis=1)
        vn = lax.dynamic_slice_in_dim(vp, n * w, 3 * w, axis=1)
        kpos = n * w - w + jnp.arange(3 * w)
        mask = band & ((kpos >= 0) & (kpos < s_len))[None, :]
        sw = jnp.einsum('bqkgd,bskd->bkgqs', qn, kn).astype(jnp.float32) * scale
        sw = jnp.where(mask, sw, NEG_INF)
        sc = jnp.einsum('bqkgd,bckd->bkgqc', qn, kc).astype(jnp.float32) * scale
        sk = jnp.broadcast_to(sink_l, sw.shape[:-1] + (1,))
        p = jax.nn.softmax(jnp.concatenate([sw, sc, sk], axis=-1), axis=-1).astype(q.dtype)
        pw = p[..., :3 * w]
        pc = p[..., 3 * w:3 * w + n_ctx]
        return (jnp.einsum('bkgqs,bskd->bqkgd', pw, vn)
                + jnp.einsum('bkgqc,bckd->bqkgd', pc, vc))

    out = lax.map(block, (jnp.arange(nb, dtype=jnp.int32), qb))
    return jnp.swapaxes(out, 0, 1).reshape(bsz, s_len, h * d)


def _mla_attend(qn, qp, kn, kpe, v):
    s = (jnp.einsum('bqhd,bkhd->bhqk', qn, kn).astype(jnp.float32)
         + jnp.einsum('bqhr,bkr->bhqk', qp, kpe).astype(jnp.float32)) * ((B_NOPE + B_ROPE) ** -0.5)
    p = jax.nn.softmax(s, axis=-1).astype(v.dtype)
    return jnp.einsum('bhqk,bkhd->bqhd', p, v)


def _mla_latent(qn, qp, kn, kpe, v, kn_c, kpe_c, v_c):
    bsz, s_len, h, _ = qn.shape
    nb = s_len // Q_BLOCK
    kn_all = jnp.concatenate([kn, kn_c], axis=1)
    kpe_all = jnp.concatenate([kpe, kpe_c], axis=1)
    v_all = jnp.concatenate([v, v_c], axis=1)
    qn_b = jnp.swapaxes(qn.reshape(bsz, nb, Q_BLOCK, h, -1), 0, 1)
    qp_b = jnp.swapaxes(qp.reshape(bsz, nb, Q_BLOCK, h, -1), 0, 1)
    out = lax.map(lambda a: _mla_attend(a[0], a[1], kn_all, kpe_all, v_all), (qn_b, qp_b))
    return jnp.swapaxes(out, 0, 1).reshape(bsz, s_len, h * B_V)


def _even_heads(h, w_in, q_norm_g, w_uq, kv_norm_g, w_ukv, rows, cols):
    bsz, n, _ = h.shape
    qa, ka, va, cq, ckv, kpe = _split_cols(h @ w_in, EVEN_SPLIT)
    qa = qa.reshape(bsz, n, A_HEADS, HEAD_DIM)
    ka = ka.reshape(bsz, n, A_KV_HEADS, HEAD_DIM)
    va = va.reshape(bsz, n, A_KV_HEADS, HEAD_DIM)
    qb = (_rmsnorm(cq, q_norm_g) @ w_uq).reshape(bsz, n, B_HEADS, B_NOPE + B_ROPE)
    qn, qp = qb[..., :B_NOPE], qb[..., B_NOPE:]
    kvb = (_rmsnorm(ckv, kv_norm_g) @ w_ukv).reshape(bsz, n, B_HEADS, B_NOPE + B_V)
    kn, vb = kvb[..., :B_NOPE], kvb[..., B_NOPE:]
    if rows is not None:
        qa = _rope2d(qa, rows, cols)
        ka = _rope2d(ka, rows, cols)
        qp = _rope2d(qp, rows, cols)
        kpe = _rope2d(kpe, rows, cols)
    return qa, ka, va, qn, qp, kn, kpe, vb


def _even_mixer(hl, hc, rows, cols, w_in, sink, q_norm_g, w_uq, kv_norm_g, w_ukv, w_out, need_ctx):
    qa, ka, va, qn, qp, kn, kpe, vb = _even_heads(hl, w_in, q_norm_g, w_uq, kv_norm_g, w_ukv, rows, cols)
    qa_c, ka_c, va_c, qn_c, qp_c, kn_c, kpe_c, vb_c = _even_heads(hc, w_in, q_norm_g, w_uq, kv_norm_g, w_ukv, None, None)
    oa = _window_attn_latent(qa, ka, va, ka_c, va_c, sink)
    ob = _mla_latent(qn, qp, kn, kpe, vb, kn_c, kpe_c, vb_c)
    out_l = jnp.concatenate([oa, ob], axis=-1) @ w_out
    if not need_ctx:
        return out_l, None
    oa_c = _attend_dense(qa_c, ka_c, va_c, sink)
    ob_c = _mla_attend(qn_c, qp_c, kn_c, kpe_c, vb_c).reshape(hc.shape[0], hc.shape[1], B_HEADS * B_V)
    out_c = jnp.concatenate([oa_c, ob_c], axis=-1) @ w_out
    return out_l, out_c


def _na_latent(q, k, v, kc, vc, rpb):
    bsz, s_len, h, d = q.shape
    n_rows = s_len // GRID_W
    kh = min(NA_ROWS, n_rows)
    kw = NA_COLS
    scale = d ** -0.5
    qg = q.reshape(bsz, n_rows, GRID_W, h, d)
    kg = k.reshape(bsz, n_rows, GRID_W, h, d)
    vg = v.reshape(bsz, n_rows, GRID_W, h, d)
    col = jnp.arange(GRID_W)
    col_idx = jnp.clip(col - kw // 2, 0, GRID_W - kw)[:, None] + jnp.arange(kw)[None, :]
    dc = col_idx - col[:, None] + (NA_COLS - 1)

    def row(args):
        r, qr = args
        r0 = jnp.clip(r - kh // 2, 0, n_rows - kh)
        kb = lax.dynamic_slice_in_dim(kg, r0, kh, axis=1)[:, :, col_idx]
        vb = lax.dynamic_slice_in_dim(vg, r0, kh, axis=1)[:, :, col_idx]
        dr = r0 + jnp.arange(kh) - r + (NA_ROWS - 1)
        bias = jnp.transpose(rpb[:, dr[:, None, None], dc[None, :, :]], (0, 2, 1, 3))
        sn = jnp.einsum('bqhd,biqjhd->bhqij', qr, kb).astype(jnp.float32) * scale + bias.astype(jnp.float32)
        sn = sn.reshape(bsz, h, GRID_W, kh * kw)
        sc = jnp.einsum('bqhd,bchd->bhqc', qr, kc).astype(jnp.float32) * scale
        p = jax.nn.softmax(jnp.concatenate([sn, sc], axis=-1), axis=-1).astype(q.dtype)
        pn = p[..., :kh * kw].reshape(bsz, h, GRID_W, kh, kw)
        pc = p[..., kh * kw:]
        return (jnp.einsum('bhqij,biqjhd->bqhd', pn, vb)
                + jnp.einsum('bhqc,bchd->bqhd', pc, vc))

    out = lax.map(row, (jnp.arange(n_rows, dtype=jnp.int32), jnp.swapaxes(qg, 0, 1)))
    return jnp.swapaxes(out, 0, 1).reshape(bsz, s_len, h * d)


def _odd_heads(h, w_in):
    bsz, n, _ = h.shape
    q, k, v = jnp.split(h @ w_in, 3, axis=-1)
    shp = (bsz, n, C_HEADS, HEAD_DIM)
    return q.reshape(shp), k.reshape(shp), v.reshape(shp)


def _odd_mixer(hl, hc, w_in, rpb, w_out, need_ctx):
    q, k, v = _odd_heads(hl, w_in)
    q_c, k_c, v_c = _odd_heads(hc, w_in)
    out_l = _na_latent(q, k, v, k_c, v_c, rpb) @ w_out
    if not need_ctx:
        return out_l, None
    return out_l, _attend_dense(q_c, k_c, v_c, None) @ w_out


def _clamped_swiglu(u):
    glu, lin = u[..., :D_EXPERT], u[..., D_EXPERT:]
    glu = jnp.minimum(glu, SWIGLU_LIMIT)
    lin = jnp.clip(lin, -SWIGLU_LIMIT, SWIGLU_LIMIT)
    return glu * jax.nn.sigmoid(SWIGLU_ALPHA * glu) * (lin + 1.0)


def _moe(h, router_w, router_b, w1, b1, w2, b2):
    n_tok, d = h.shape
    logits = (h @ router_w + router_b).astype(jnp.float32)
    top_val, top_exp = lax.top_k(logits, TOP_K)
    gates = jax.nn.softmax(top_val, axis=-1)
    n_asg = n_tok * TOP_K
    flat_e = top_exp.reshape(n_asg)
    order = jnp.argsort(flat_e)
    sorted_e = flat_e[order]
    counts = jnp.zeros((N_EXPERTS,), jnp.int32).at[flat_e].add(1)
    padded = (counts + MOE_BLOCK - 1) // MOE_BLOCK * MOE_BLOCK
    pad_end = jnp.cumsum(padded)
    pad_start = pad_end - padded
    grp_start = jnp.cumsum(counts) - counts
    dest = pad_start[sorted_e] + jnp.arange(n_asg, dtype=jnp.int32) - grp_start[sorted_e]
    n_blk = (n_asg + N_EXPERTS * (MOE_BLOCK - 1) + MOE_BLOCK - 1) // MOE_BLOCK
    n_slot = n_blk * MOE_BLOCK
    buf_tok = jnp.full((n_slot,), n_tok, jnp.int32).at[dest].set((order // TOP_K).astype(jnp.int32))
    buf_gate = jnp.zeros((n_slot,), jnp.float32).at[dest].set(gates.reshape(n_asg)[order])
    blk_exp = jnp.minimum(jnp.searchsorted(pad_end, jnp.arange(n_blk, dtype=jnp.int32) * MOE_BLOCK, side='right'), N_EXPERTS - 1)
    h_pad = jnp.concatenate([h, jnp.zeros((1, d), h.dtype)], axis=0)

    def block(args):
        tok, gate, e = args
        u = h_pad[tok] @ w1[e] + b1[e]
        y = _clamped_swiglu(u) @ w2[e] + b2[e]
        return y * gate[:, None].astype(y.dtype)

    ys = lax.map(block, (buf_tok.reshape(n_blk, MOE_BLOCK), buf_gate.reshape(n_blk, MOE_BLOCK), blk_exp))
    return jnp.zeros((n_tok + 1, d), h.dtype).at[buf_tok].add(ys.reshape(n_slot, d))[:n_tok]


def setup_inputs(seed: int = 0) -> dict:
    key = jax.random.key(seed)
    ks = jax.random.split(key, 25)
    d = D_MODEL

    def nrm(k, shape, std):
        return jax.random.normal(k, shape, jnp.float32) * std

    return {
        'x': nrm(ks[0], (BATCH, SEQ, d), 1.0),
        'c': nrm(ks[1], (BATCH, d), 1.0),
        'ctx': nrm(ks[2], (BATCH, CTX_LEN, d), 1.0),
        'c_ctx': nrm(ks[3], (d,), 1.0),
        'ada_w': nrm(ks[4], (DEPTH, d, 6 * d), 0.5 * d ** -0.5),
        'ada_b': nrm(ks[5], (DEPTH, 6 * d), 0.02),
        'norm1_g': 1.0 + nrm(ks[6], (DEPTH, d), 0.05),
        'norm2_g': 1.0 + nrm(ks[7], (DEPTH, d), 0.05),
        'ev_w_in': nrm(ks[8], (N_EVEN, d, EVEN_IN), d ** -0.5),
        'ev_sink': nrm(ks[9], (N_EVEN, A_HEADS), 1.0),
        'ev_q_norm_g': 1.0 + nrm(ks[10], (N_EVEN, B_Q_RANK), 0.05),
        'ev_w_uq': nrm(ks[11], (N_EVEN, B_Q_RANK, B_HEADS * (B_NOPE + B_ROPE)), B_Q_RANK ** -0.5),
        'ev_kv_norm_g': 1.0 + nrm(ks[12], (N_EVEN, B_KV_RANK), 0.05),
        'ev_w_ukv': nrm(ks[13], (N_EVEN, B_KV_RANK, B_HEADS * (B_NOPE + B_V)), B_KV_RANK ** -0.5),
        'ev_w_out': nrm(ks[14], (N_EVEN, MIX_WIDTH, d), MIX_WIDTH ** -0.5),
        'od_w_in': nrm(ks[15], (N_ODD, d, ODD_IN), d ** -0.5),
        'od_rpb': nrm(ks[16], (N_ODD, C_HEADS, 2 * NA_ROWS - 1, 2 * NA_COLS - 1), 0.5),
        'od_w_out': nrm(ks[17], (N_ODD, ODD_WIDTH, d), ODD_WIDTH ** -0.5),
        'router_w': nrm(ks[18], (DEPTH, d, N_EXPERTS), d ** -0.5),
        'router_b': nrm(ks[19], (DEPTH, N_EXPERTS), 0.01),
        'exp_w1': nrm(ks[20], (DEPTH, N_EXPERTS, d, 2 * D_EXPERT), d ** -0.5),
        'exp_b1': nrm(ks[21], (DEPTH, N_EXPERTS, 2 * D_EXPERT), 0.02),
        'exp_w2': nrm(ks[22], (DEPTH, N_EXPERTS, D_EXPERT, d), D_EXPERT ** -0.5),
        'exp_b2': nrm(ks[23], (DEPTH, N_EXPERTS, d), 0.02),
        'final_g': 1.0 + nrm(ks[24], (d,), 0.05),
    }


def reference(x, c, ctx, c_ctx, ada_w, ada_b, norm1_g, norm2_g, ev_w_in, ev_sink, ev_q_norm_g, ev_w_uq,
              ev_kv_norm_g, ev_w_ukv, ev_w_out, od_w_in, od_rpb, od_w_out, router_w, router_b,
              exp_w1, exp_b1, exp_w2, exp_b2, final_g):
    bsz, seq, d = x.shape
    n_ctx = ctx.shape[1]
    t = jnp.arange(seq, dtype=jnp.int32)
    rows, cols = t // GRID_W, t % GRID_W
    silu_c = jax.nn.silu(c)
    silu_cc = jax.nn.silu(c_ctx)[None, :]
    xl, xc = x, ctx
    for layer in range(DEPTH):
        last = layer == DEPTH - 1
        i = layer // 2
        mod_l = jnp.split((silu_c @ ada_w[layer] + ada_b[layer])[:, None, :], 6, axis=-1)
        mod_c = jnp.split((silu_cc @ ada_w[layer] + ada_b[layer])[:, None, :], 6, axis=-1)
        hl = _modulate(xl, norm1_g[layer], mod_l[0], mod_l[1])
        hc = _modulate(xc, norm1_g[layer], mod_c[0], mod_c[1])
        if layer % 2 == 0:
            ml, mc = _even_mixer(hl, hc, rows, cols, ev_w_in[i], ev_sink[i], ev_q_norm_g[i], ev_w_uq[i],
                                 ev_kv_norm_g[i], ev_w_ukv[i], ev_w_out[i], not last)
        else:
            ml, mc = _odd_mixer(hl, hc, od_w_in[i], od_rpb[i], od_w_out[i], not last)
        xl = xl + mod_l[2] * ml
        hl = _modulate(xl, norm2_g[layer], mod_l[3], mod_l[4]).reshape(bsz * seq, d)
        if last:
            y = _moe(hl, router_w[layer], router_b[layer], exp_w1[layer], exp_b1[layer], exp_w2[layer], exp_b2[layer])
            xl = xl + mod_l[5] * y.reshape(bsz, seq, d)
        else:
            xc = xc + mod_c[2] * mc
            hc = _modulate(xc, norm2_g[layer], mod_c[3], mod_c[4]).reshape(bsz * n_ctx, d)
            y = _moe(jnp.concatenate([hl, hc], axis=0), router_w[layer], router_b[layer], exp_w1[layer],
                     exp_b1[layer], exp_w2[layer], exp_b2[layer])
            xl = xl + mod_l[5] * y[:bsz * seq].reshape(bsz, seq, d)
            xc = xc + mod_c[5] * y[bsz * seq:].reshape(bsz, n_ctx, d)
    return _rmsnorm(xl, final_g)
```

```python
import functools

import numpy as np
import jax
import jax.numpy as jnp
from jax import lax
from jax.experimental import pallas as pl
from jax.experimental.pallas import tpu as pltpu

GRID_W = 64
HEAD_DIM = 64
ROPE_BASE = 10000.0
NORM_EPS = 1e-6
NEG_INF = -1e30

A_HEADS = 8
A_KV_HEADS = 2
A_WINDOW = 128
B_HEADS = 8
B_NOPE = 64
B_ROPE = 32
B_V = 64
B_Q_RANK = 768
B_KV_RANK = 256
C_HEADS = 16
NA_ROWS = 8
NA_COLS = 16

N_EXPERTS = 32
TOP_K = 4
D_EXPERT = 1024
SWIGLU_LIMIT = 7.0
SWIGLU_ALPHA = 1.702

LANES = 128
MOD_ROWS = 8
VMEM_LIMIT = 48 * 1024 * 1024

BF16 = jnp.bfloat16
F32 = jnp.float32
NT_DIMS = (((1,), (1,)), ((), ()))


def _cparams(*sem):
    return pltpu.CompilerParams(dimension_semantics=sem, vmem_limit_bytes=VMEM_LIMIT)


def _dot(a, b):
    return jnp.dot(a, b, preferred_element_type=F32)


def _dot_nt(a, b):
    return lax.dot_general(a, b, NT_DIMS, preferred_element_type=F32)


def _ada_kernel(c_ref, w_ref, b_ref, o_ref):
    c = c_ref[...]
    s = c / (1.0 + jnp.exp(-c))
    o_ref[0] = jnp.dot(s, w_ref[0], preferred_element_type=F32, precision=lax.Precision.HIGHEST) + b_ref[0]


def _ada_modulation(c_rows, ada_w, ada_b):
    depth, d, n = ada_w.shape
    tn = 1024
    return pl.pallas_call(
        _ada_kernel,
        grid=(depth, n // tn),
        in_specs=[
            pl.BlockSpec((MOD_ROWS, d), lambda l, j: (0, 0)),
            pl.BlockSpec((1, d, tn), lambda l, j: (l, 0, j)),
            pl.BlockSpec((1, 1, tn), lambda l, j: (l, 0, j)),
        ],
        out_specs=pl.BlockSpec((1, MOD_ROWS, tn), lambda l, j: (l, 0, j)),
        out_shape=jax.ShapeDtypeStruct((depth, MOD_ROWS, n), F32),
        compiler_params=_cparams("parallel", "parallel"),
        name="ada_modulation",
    )(c_rows, ada_w, ada_b.reshape(depth, 1, n))


def _norm_mm_kernel(*refs, modulate):
    if modulate:
        x_ref, g_ref, sh_ref, sc_ref, w_ref, o_ref = refs
    else:
        x_ref, g_ref, w_ref, o_ref = refs
    x = x_ref[...].astype(F32)
    ms = jnp.mean(x * x, axis=-1, keepdims=True)
    h = x * lax.rsqrt(ms + NORM_EPS) * g_ref[...]
    if modulate:
        h = h * (1.0 + sc_ref[0]) + sh_ref[0]
    o_ref[...] = _dot(h.astype(BF16), w_ref[...]).astype(o_ref.dtype)


def _row_tile(t):
    return 512 if t % 512 == 0 else 256


def _norm_matmul(x, col_block, kdim, g, w, mod=None, shift_idx=0, scale_idx=0, seg_of_tile=None):
    t = x.shape[0]
    n = w.shape[1]
    tm = _row_tile(t)
    in_specs = [pl.BlockSpec((tm, kdim), lambda i: (i, col_block)),
                pl.BlockSpec((1, kdim), lambda i: (0, 0))]
    args = [x, g.reshape(1, kdim).astype(F32)]
    if mod is not None:
        in_specs += [pl.BlockSpec((1, 1, kdim), lambda i: (seg_of_tile(i, tm) * 6 + shift_idx, 0, 0)),
                     pl.BlockSpec((1, 1, kdim), lambda i: (seg_of_tile(i, tm) * 6 + scale_idx, 0, 0))]
        args += [mod, mod]
    in_specs.append(pl.BlockSpec((kdim, n), lambda i: (0, 0)))
    args.append(w)
    return pl.pallas_call(
        functools.partial(_norm_mm_kernel, modulate=mod is not None),
        grid=(t // tm,),
        in_specs=in_specs,
        out_specs=pl.BlockSpec((tm, n), lambda i: (i, 0)),
        out_shape=jax.ShapeDtypeStruct((t, n), BF16),
        compiler_params=_cparams("parallel"),
        name="norm_matmul",
    )(*args)


def _out_router_kernel(*refs, n_a):
    a_refs = refs[:n_a]
    w_refs = refs[n_a:2 * n_a]
    x_ref, gate_ref, g2_ref, sh_ref, sc_ref, rwt_ref, rb_ref = refs[2 * n_a:2 * n_a + 7]
    xo_ref, h_ref, ti_ref, tg_ref = refs[2 * n_a + 7:]
    acc = _dot(a_refs[0][...], w_refs[0][...])
    for k in range(1, n_a):
        acc = acc + _dot(a_refs[k][...], w_refs[k][...])
    xn = x_ref[...] + gate_ref[0] * acc
    xo_ref[...] = xn
    ms = jnp.mean(xn * xn, axis=-1, keepdims=True)
    h = xn * lax.rsqrt(ms + NORM_EPS) * g2_ref[...]
    h = h * (1.0 + sc_ref[0]) + sh_ref[0]
    h_ref[...] = h.astype(BF16)
    logits = lax.dot_general(rwt_ref[...], h, NT_DIMS, preferred_element_type=F32,
                             precision=lax.Precision.HIGHEST) + rb_ref[...]
    eidx = lax.broadcasted_iota(jnp.int32, logits.shape, 0)
    vals, idxs = [], []
    cur = logits
    for _ in range(TOP_K):
        m = jnp.max(cur, axis=0, keepdims=True)
        idx = jnp.min(jnp.where(cur == m, eidx, N_EXPERTS), axis=0, keepdims=True)
        vals.append(m)
        idxs.append(idx)
        cur = jnp.where(eidx == idx, -jnp.inf, cur)
    es = [jnp.exp(v - vals[0]) for v in vals]
    den = es[0] + es[1] + es[2] + es[3]
    ti_ref[...] = jnp.concatenate(idxs, axis=0)
    tg_ref[...] = jnp.concatenate([e / den for e in es], axis=0)


def _out_router(a_list, w_list, x, mod, g2, router_w, router_b, seg_of_tile):
    t, d = x.shape
    tm = _row_tile(t)
    n_a = len(a_list)

    def mod_spec(idx):
        return pl.BlockSpec((1, 1, d), lambda i: (seg_of_tile(i, tm) * 6 + idx, 0, 0))

    in_specs = [pl.BlockSpec((tm, a.shape[1]), lambda i: (i, 0)) for a in a_list]
    in_specs += [pl.BlockSpec(w.shape, lambda i: (0, 0)) for w in w_list]
    in_specs += [pl.BlockSpec((tm, d), lambda i: (i, 0)), mod_spec(2),
                 pl.BlockSpec((1, d), lambda i: (0, 0)), mod_spec(3), mod_spec(4),
                 pl.BlockSpec((N_EXPERTS, d), lambda i: (0, 0)),
                 pl.BlockSpec((N_EXPERTS, 1), lambda i: (0, 0))]
    return pl.pallas_call(
        functools.partial(_out_router_kernel, n_a=n_a),
        grid=(t // tm,),
        in_specs=in_specs,
        out_specs=[pl.BlockSpec((tm, d), lambda i: (i, 0)),
                   pl.BlockSpec((tm, d), lambda i: (i, 0)),
                   pl.BlockSpec((TOP_K, tm), lambda i: (0, i)),
                   pl.BlockSpec((TOP_K, tm), lambda i: (0, i))],
        out_shape=[jax.ShapeDtypeStruct((t, d), F32),
                   jax.ShapeDtypeStruct((t, d), BF16),
                   jax.ShapeDtypeStruct((TOP_K, t), jnp.int32),
                   jax.ShapeDtypeStruct((TOP_K, t), F32)],
        compiler_params=_cparams("parallel"),
        name="out_router",
    )(*a_list, *w_list, x, mod, g2.reshape(1, d), mod, mod,
      router_w.T, router_b.reshape(N_EXPERTS, 1))


def _half_mask(shape, j):
    lane = lax.broadcasted_iota(jnp.int32, shape, 1)
    return (lane >= HEAD_DIM * j) & (lane < HEAD_DIM * (j + 1))


def _softmax_pv(scores, values, sink=None):
    m = jnp.max(scores[0], axis=1, keepdims=True)
    for s in scores[1:]:
        m = jnp.maximum(m, jnp.max(s, axis=1, keepdims=True))
    if sink is not None:
        m = jnp.maximum(m, sink)
    den = None
    out = None
    for s, v in zip(scores, values):
        p = jnp.exp(s - m)
        ps = jnp.sum(p, axis=1, keepdims=True)
        den = ps if den is None else den + ps
        o = _dot(p.astype(BF16), v)
        out = o if out is None else out + o
    if sink is not None:
        den = den + jnp.exp(sink - m)
    return out / den


def _pair_attn_kernel(*refs, window, tq, s_len, has_sink, heads_per_group):
    if window:
        q_ref, kl_ref, vl_ref, kc_ref, vc_ref = refs[:5]
        rest = refs[5:]
    else:
        q_ref, kc_ref, vc_ref = refs[:3]
        rest = refs[3:]
    if has_sink:
        sink_ref, o_ref = rest
    else:
        (o_ref,) = rest
    blk = pl.program_id(1)
    q = q_ref[...]
    kc = kc_ref[...]
    vc = vc_ref[...]
    if window:
        i = pl.program_id(2)
        wl = tq + 2 * A_WINDOW
        start = pl.multiple_of(jnp.clip(i * tq - A_WINDOW, 0, s_len - wl), LANES)
        kw = kl_ref[pl.ds(start, wl), :]
        vw = vl_ref[pl.ds(start, wl), :]
        qpos = i * tq + lax.broadcasted_iota(jnp.int32, (tq, wl), 0)
        kpos = start + lax.broadcasted_iota(jnp.int32, (tq, wl), 1)
        band = jnp.abs(qpos - kpos) <= A_WINDOW
    outs = []
    for j in range(2):
        qj = jnp.where(_half_mask(q.shape, j), q, jnp.zeros_like(q))
        scores, values = [], []
        if window:
            scores.append(jnp.where(band, _dot_nt(qj, kw), NEG_INF))
            values.append(vw)
        scores.append(_dot_nt(qj, kc))
        values.append(vc)
        sink = sink_ref[j * heads_per_group + blk] if has_sink else None
        outs.append(_softmax_pv(scores, values, sink))
    o_ref[...] = jnp.where(_half_mask(outs[0].shape, 0), outs[0], outs[1]).astype(o_ref.dtype)


def _pair_attn(q_arr, q_cb, k_arr, k_cb, v_arr, v_cb, n_blk, bsz, s_len, n_ctx, sink, window):
    ctx_blk0 = bsz * s_len // n_ctx
    if window:
        tq = 512
        nq = s_len // tq
        q_spec = pl.BlockSpec((tq, LANES), lambda b, h, i: (b * nq + i, q_cb(h)))
        kv_specs = [pl.BlockSpec((s_len, LANES), lambda b, h, i: (b, k_cb(h))),
                    pl.BlockSpec((s_len, LANES), lambda b, h, i: (b, v_cb(h)))]
        args = [q_arr, k_arr, v_arr, k_arr, v_arr]
        out_rows = bsz * s_len
    else:
        tq = n_ctx
        nq = 1
        q_spec = pl.BlockSpec((tq, LANES), lambda b, h, i: (ctx_blk0 + b, q_cb(h)))
        kv_specs = []
        args = [q_arr, k_arr, v_arr]
        out_rows = bsz * n_ctx
    kv_specs += [pl.BlockSpec((n_ctx, LANES), lambda b, h, i: (ctx_blk0 + b, k_cb(h))),
                 pl.BlockSpec((n_ctx, LANES), lambda b, h, i: (ctx_blk0 + b, v_cb(h)))]
    in_specs = [q_spec] + kv_specs
    if sink is not None:
        in_specs.append(pl.BlockSpec(memory_space=pltpu.SMEM))
        args.append(sink.astype(F32))
    return pl.pallas_call(
        functools.partial(_pair_attn_kernel, window=window, tq=tq, s_len=s_len,
                          has_sink=sink is not None, heads_per_group=n_blk),
        grid=(bsz, n_blk, nq),
        in_specs=in_specs,
        out_specs=pl.BlockSpec((tq, LANES), lambda b, h, i: (b * nq + i, h)),
        out_shape=jax.ShapeDtypeStruct((out_rows, n_blk * LANES), BF16),
        compiler_params=_cparams("parallel", "parallel", "arbitrary"),
        name="pair_attn_window" if window else "pair_attn_ctx",
    )(*args)


def _mla_kernel(*refs, latent, tk, n_chunks):
    if latent:
        q_ref, kl_ref, vl_ref, kc_ref, vc_ref, o_ref = refs
    else:
        q_ref, kc_ref, vc_ref, o_ref = refs
    vc = vc_ref[...]
    outs = []
    for j in range(2):
        q = q_ref[:, LANES * j:LANES * (j + 1)]
        s = _dot_nt(q, kc_ref[:, LANES * j:LANES * (j + 1)])
        m = jnp.max(s, axis=1, keepdims=True)
        p = jnp.exp(s - m)
        l = jnp.sum(p, axis=1, keepdims=True)
        acc = _dot(p.astype(BF16), vc)
        if latent:
            def body(c, carry, q=q, j=j):
                m, l, acc = carry
                off = pl.multiple_of(c * tk, tk)
                k = kl_ref[pl.ds(off, tk), LANES * j:LANES * (j + 1)]
                v = vl_ref[pl.ds(off, tk), :]
                s = _dot_nt(q, k)
                m_new = jnp.maximum(m, jnp.max(s, axis=1, keepdims=True))
                p = jnp.exp(s - m_new)
                alpha = jnp.exp(m - m_new)
                l_new = alpha * l + jnp.sum(p, axis=1, keepdims=True)
                acc_new = alpha * acc + _dot(p.astype(BF16), v)
                return m_new, l_new, acc_new
            m, l, acc = lax.fori_loop(0, n_chunks, body, (m, l, acc))
        outs.append(acc / l)
    o_ref[...] = jnp.where(_half_mask(outs[0].shape, 0), outs[0], outs[1]).astype(o_ref.dtype)


def _mla_attn(q_arr, k_arr, v_arr, bsz, s_len, n_ctx, latent):
    n_pair = B_HEADS // 2
    ctx_blk0 = bsz * s_len // n_ctx
    kv_specs = [pl.BlockSpec((n_ctx, 2 * LANES), lambda b, h, i: (ctx_blk0 + b, h)),
                pl.BlockSpec((n_ctx, LANES), lambda b, h, i: (ctx_blk0 + b, h))]
    if latent:
        tq, tk = 256, 512
        nq = s_len // tq
        q_spec = pl.BlockSpec((tq, 2 * LANES), lambda b, h, i: (b * nq + i, h))
        kv_specs = [pl.BlockSpec((s_len, 2 * LANES), lambda b, h, i: (b, h)),
                    pl.BlockSpec((s_len, LANES), lambda b, h, i: (b, h))] + kv_specs
        args = [q_arr, k_arr, v_arr, k_arr, v_arr]
        out_rows = bsz * s_len
    else:
        tq, tk = n_ctx, n_ctx
        nq = 1
        q_spec = pl.BlockSpec((tq, 2 * LANES), lambda b, h, i: (ctx_blk0 + b, h))
        args = [q_arr, k_arr, v_arr]
        out_rows = bsz * n_ctx
    return pl.pallas_call(
        functools.partial(_mla_kernel, latent=latent, tk=tk, n_chunks=s_len // tk),
        grid=(bsz, n_pair, nq),
        in_specs=[q_spec] + kv_specs,
        out_specs=pl.BlockSpec((tq, LANES), lambda b, h, i: (b * nq + i, h)),
        out_shape=jax.ShapeDtypeStruct((out_rows, n_pair * LANES), BF16),
        compiler_params=_cparams("parallel", "parallel", "arbitrary"),
        name="mla_latent" if latent else "mla_ctx",
    )(*args)


NA_Q_ROWS = 4
NA_K_ROWS = NA_Q_ROWS + NA_ROWS


def _na_kernel(pat_ref, start_ref, q_ref, kl_ref, vl_ref, kc_ref, vc_ref, bias_ref, o_ref):
    del pat_ref
    rb = pl.program_id(2)
    nk = NA_K_ROWS * GRID_W
    start = pl.multiple_of(start_ref[rb] * GRID_W, NA_Q_ROWS * GRID_W)
    kw = kl_ref[pl.ds(start, nk), :]
    vw = vl_ref[pl.ds(start, nk), :]
    kc = kc_ref[...]
    vc = vc_ref[...]
    q = q_ref[...]
    outs = []
    for j in range(2):
        qj = jnp.where(_half_mask(q.shape, j), q, jnp.zeros_like(q))
        sw = _dot_nt(qj, kw) + bias_ref[0, j]
        sc = _dot_nt(qj, kc)
        outs.append(_softmax_pv([sw, sc], [vw, vc]))
    o_ref[...] = jnp.where(_half_mask(outs[0].shape, 0), outs[0], outs[1]).astype(o_ref.dtype)


def _na_patterns(n_rows):
    kh = NA_ROWS
    n_rb = n_rows // NA_Q_ROWS
    starts, keys = [], []
    for rb in range(n_rb):
        r_a = rb * NA_Q_ROWS
        start = int(np.clip(r_a - NA_Q_ROWS, 0, n_rows - NA_K_ROWS))
        assert start % NA_Q_ROWS == 0
        rows = r_a + np.arange(NA_Q_ROWS)
        r0 = np.clip(rows - kh // 2, 0, n_rows - kh)
        assert start <= r0.min() and r0.max() + kh <= start + NA_K_ROWS
        starts.append(start)
        keys.append((r_a - start, tuple((r0 - start).tolist())))
    uniq = sorted(set(keys))
    pat = [uniq.index(k) for k in keys]
    return np.asarray(starts, np.int32), np.asarray(pat, np.int32), uniq


def _na_bias_table(rpb, uniq):
    n_dr, n_dc = 2 * NA_ROWS - 1, 2 * NA_COLS - 1
    i = np.arange(NA_Q_ROWS)[:, None]
    j = np.arange(NA_K_ROWS)[None, :]
    rsel = np.zeros((len(uniq), NA_Q_ROWS, NA_K_ROWS, n_dr), np.float32)
    rvalid = np.zeros((len(uniq), NA_Q_ROWS, NA_K_ROWS), bool)
    for p, (delta, r0_rel) in enumerate(uniq):
        r0_rel = np.asarray(r0_rel)[:, None]
        valid = (j >= r0_rel) & (j < r0_rel + NA_ROWS)
        dr = np.clip(j - (delta + i) + NA_ROWS - 1, 0, n_dr - 1)
        rsel[p] = np.eye(n_dr, dtype=np.float32)[dr] * valid[..., None]
        rvalid[p] = valid
    c = np.arange(GRID_W)[:, None]
    kc = np.arange(GRID_W)[None, :]
    c0 = np.clip(c - NA_COLS // 2, 0, GRID_W - NA_COLS)
    cvalid = (kc >= c0) & (kc < c0 + NA_COLS)
    dc = np.clip(kc - c + NA_COLS - 1, 0, n_dc - 1)
    csel = np.eye(n_dc, dtype=np.float32)[dc] * cvalid[..., None]
    hp = lax.Precision.HIGHEST
    tmp = jnp.einsum('hab,cqb->hacq', rpb.astype(F32), jnp.asarray(csel), precision=hp)
    bias = jnp.einsum('pija,hacq->phicjq', jnp.asarray(rsel), tmp, precision=hp)
    valid = rvalid[:, None, :, None, :, None] & cvalid[None, None, None, :, None, :]
    bias = jnp.where(jnp.asarray(valid), bias, NEG_INF)
    return bias.reshape(len(uniq), C_HEADS, NA_Q_ROWS * GRID_W, NA_K_ROWS * GRID_W)


def _na_attn(qkv, rpb, bsz, s_len, n_ctx):
    n_pair = C_HEADS // 2
    n_rows = s_len // GRID_W
    starts, pat, uniq = _na_patterns(n_rows)
    bias = _na_bias_table(rpb, uniq)
    n_rb = n_rows // NA_Q_ROWS
    tq = NA_Q_ROWS * GRID_W
    nk = NA_K_ROWS * GRID_W
    ctx_blk0 = bsz * s_len // n_ctx
    grid_spec = pltpu.PrefetchScalarGridSpec(
        num_scalar_prefetch=2,
        grid=(n_pair, bsz, n_rb),
        in_specs=[
            pl.BlockSpec((tq, LANES), lambda h, b, r, pat, st: (b * n_rb + r, h)),
            pl.BlockSpec((s_len, LANES), lambda h, b, r, pat, st: (b, n_pair + h)),
            pl.BlockSpec((s_len, LANES), lambda h, b, r, pat, st: (b, 2 * n_pair + h)),
            pl.BlockSpec((n_ctx, LANES), lambda h, b, r, pat, st: (ctx_blk0 + b, n_pair + h)),
            pl.BlockSpec((n_ctx, LANES), lambda h, b, r, pat, st: (ctx_blk0 + b, 2 * n_pair + h)),
            pl.BlockSpec((1, 2, tq, nk), lambda h, b, r, pat, st: (pat[r], h, 0, 0)),
        ],
        out_specs=pl.BlockSpec((tq, LANES), lambda h, b, r, pat, st: (b * n_rb + r, h)),
    )
    return pl.pallas_call(
        _na_kernel,
        grid_spec=grid_spec,
        out_shape=jax.ShapeDtypeStruct((bsz * s_len, n_pair * LANES), BF16),
        compiler_params=_cparams("parallel", "parallel", "arbitrary"),
        name="na_latent",
    )(jnp.asarray(pat), jnp.asarray(starts), qkv, qkv, qkv, qkv, qkv, bias)


MOE_BLOCK = 512
MOE_FC = 512


def _moe_kernel(be_ref, x_ref, w1_ref, b1_ref, w2_ref, b2_ref, o_ref):
    del be_ref
    x = x_ref[...]
    acc = None
    for c in range(D_EXPERT // MOE_FC):
        lo, hi = c * MOE_FC, (c + 1) * MOE_FC
        glu = _dot(x, w1_ref[0, :, lo:hi]) + b1_ref[0, :, lo:hi]
        lin = _dot(x, w1_ref[0, :, D_EXPERT + lo:D_EXPERT + hi]) + b1_ref[0, :, D_EXPERT + lo:D_EXPERT + hi]
        glu = jnp.minimum(glu, SWIGLU_LIMIT)
        lin = jnp.clip(lin, -SWIGLU_LIMIT, SWIGLU_LIMIT)
        act = glu * (1.0 / (1.0 + jnp.exp(-SWIGLU_ALPHA * glu))) * (lin + 1.0)
        y = _dot(act.astype(BF16), w2_ref[0, lo:hi, :])
        acc = y if acc is None else acc + y
    o_ref[...] = acc + b2_ref[0]


def _moe_experts(xs, blk_exp, w1, b1, w2, b2):
    n_slot, d = xs.shape
    n_blk = n_slot // MOE_BLOCK
    grid_spec = pltpu.PrefetchScalarGridSpec(
        num_scalar_prefetch=1,
        grid=(n_blk,),
        in_specs=[
            pl.BlockSpec((MOE_BLOCK, d), lambda i, be: (i, 0)),
            pl.BlockSpec((1, d, 2 * D_EXPERT), lambda i, be: (be[i], 0, 0)),
            pl.BlockSpec((1, 1, 2 * D_EXPERT), lambda i, be: (be[i], 0, 0)),
            pl.BlockSpec((1, D_EXPERT, d), lambda i, be: (be[i], 0, 0)),
            pl.BlockSpec((1, 1, d), lambda i, be: (be[i], 0, 0)),
        ],
        out_specs=pl.BlockSpec((MOE_BLOCK, d), lambda i, be: (i, 0)),
    )
    return pl.pallas_call(
        _moe_kernel,
        grid_spec=grid_spec,
        out_shape=jax.ShapeDtypeStruct((n_slot, d), F32),
        compiler_params=_cparams("arbitrary"),
        name="moe_experts",
    )(blk_exp, xs, w1, b1.reshape(N_EXPERTS, 1, 2 * D_EXPERT), w2, b2.reshape(N_EXPERTS, 1, d))


def _moe(h, top_idx_t, top_gate_t, w1, b1, w2, b2):
    t = h.shape[0]
    n_asg = t * TOP_K
    flat_e = top_idx_t.T.reshape(n_asg)
    gates = top_gate_t.T
    order = jnp.argsort(flat_e, stable=True).astype(jnp.int32)
    sorted_e = flat_e[order]
    counts = jnp.sum((flat_e[:, None] == jnp.arange(N_EXPERTS, dtype=jnp.int32)[None, :]).astype(jnp.int32), axis=0)
    padded = (counts + MOE_BLOCK - 1) // MOE_BLOCK * MOE_BLOCK
    pad_end = jnp.cumsum(padded)
    pad_start = pad_end - padded
    grp_start = jnp.cumsum(counts) - counts
    dest = pad_start[sorted_e] + jnp.arange(n_asg, dtype=jnp.int32) - grp_start[sorted_e]
    n_blk = (n_asg + N_EXPERTS * (MOE_BLOCK - 1) + MOE_BLOCK - 1) // MOE_BLOCK
    n_slot = n_blk * MOE_BLOCK
    buf_tok = jnp.zeros((n_slot,), jnp.int32).at[dest].set(order // TOP_K)
    slot = jnp.zeros((n_asg,), jnp.int32).at[order].set(dest).reshape(t, TOP_K)
    blk_exp = jnp.minimum(
        jnp.searchsorted(pad_end, jnp.arange(n_blk, dtype=jnp.int32) * MOE_BLOCK, side='right'),
        N_EXPERTS - 1).astype(jnp.int32)
    ys = _moe_experts(h[buf_tok], blk_exp, w1, b1, w2, b2)
    return jnp.sum(ys[slot] * gates[:, :, None], axis=1)


def _final_norm_kernel(x_ref, g_ref, o_ref):
    x = x_ref[...]
    ms = jnp.mean(x * x, axis=-1, keepdims=True)
    o_ref[...] = x * lax.rsqrt(ms + NORM_EPS) * g_ref[...]


def _final_norm(x, g, rows):
    d = x.shape[1]
    tm = _row_tile(rows)
    return pl.pallas_call(
        _final_norm_kernel,
        grid=(rows // tm,),
        in_specs=[pl.BlockSpec((tm, d), lambda i: (i, 0)), pl.BlockSpec((1, d), lambda i: (0, 0))],
        out_specs=pl.BlockSpec((tm, d), lambda i: (i, 0)),
        out_shape=jax.ShapeDtypeStruct((rows, d), F32),
        compiler_params=_cparams("parallel"),
        name="final_norm",
    )(x, g.reshape(1, d))


def _rope_1d(x, pos):
    half = x.shape[-1] // 2
    inv_freq = ROPE_BASE ** (-(jnp.arange(half, dtype=F32) / half))
    ang = pos.astype(F32)[:, None] * inv_freq
    cos = jnp.cos(ang)[:, None, :]
    sin = jnp.sin(ang)[:, None, :]
    x1, x2 = x[..., :half], x[..., half:]
    return jnp.concatenate([x1 * cos - x2 * sin, x2 * cos + x1 * sin], axis=-1)


def _rope2d(x, rows, cols):
    h = x.shape[-1] // 2
    xf = x.astype(F32)
    return jnp.concatenate([_rope_1d(xf[..., :h], rows), _rope_1d(xf[..., h:], cols)], axis=-1).astype(x.dtype)


def _pad_cols(w, n):
    return jnp.pad(w, ((0, 0), (0, n - w.shape[1])))


A_HEAD_ORDER = tuple(h for blk in range(A_HEADS // 2) for h in (blk, blk + A_HEADS // 2))


def _even_layer_attn(x, mod, seg_of_tile, norm1_g, w_in, sink, q_norm_g, w_uq, kv_norm_g, w_ukv, w_out,
                     rows, cols, bsz, s_len, n_ctx):
    d = x.shape[1]
    t = x.shape[0]
    n_lat = bsz * s_len
    sizes = (A_HEADS * HEAD_DIM, A_KV_HEADS * HEAD_DIM, A_KV_HEADS * HEAD_DIM, B_Q_RANK, B_KV_RANK, B_ROPE)
    offs = np.cumsum((0,) + sizes)
    w_qa = w_in[:, offs[0]:offs[1]].reshape(d, A_HEADS, HEAD_DIM)[:, A_HEAD_ORDER, :].reshape(d, -1) * (HEAD_DIM ** -0.5)
    w1 = jnp.concatenate([w_qa, w_in[:, offs[1]:]], axis=1)
    n1 = -(-w1.shape[1] // LANES) * LANES
    p1 = _norm_matmul(x, 0, d, norm1_g, _pad_cols(w1, n1).astype(BF16), mod, 0, 1, seg_of_tile)

    qa = _rope2d(p1[:, offs[0]:offs[1]].reshape(t, A_HEADS, HEAD_DIM), rows, cols).reshape(t, -1)
    ka = _rope2d(p1[:, offs[1]:offs[2]].reshape(t, A_KV_HEADS, HEAD_DIM), rows, cols).reshape(t, -1)
    va = p1[:, offs[2]:offs[3]]
    kpe = _rope2d(p1[:, offs[5]:offs[6]].reshape(t, 1, B_ROPE), rows, cols)

    qk_dim = B_NOPE + B_ROPE
    w_q = jnp.pad(w_uq.reshape(B_Q_RANK, B_HEADS, qk_dim) * (qk_dim ** -0.5),
                  ((0, 0), (0, 0), (0, LANES - qk_dim))).reshape(B_Q_RANK, B_HEADS * LANES)
    qb = _norm_matmul(p1, offs[3] // B_Q_RANK, B_Q_RANK, q_norm_g, w_q.astype(BF16)).reshape(t, B_HEADS, LANES)
    qb = jnp.concatenate([qb[..., :B_NOPE], _rope2d(qb[..., B_NOPE:qk_dim], rows, cols), qb[..., qk_dim:]],
                         axis=-1).reshape(t, B_HEADS * LANES)
    w_kv = w_ukv.reshape(B_KV_RANK, B_HEADS, B_NOPE + B_V)
    w_k = jnp.pad(w_kv[..., :B_NOPE], ((0, 0), (0, 0), (0, LANES - B_NOPE))).reshape(B_KV_RANK, B_HEADS * LANES)
    w_v = w_kv[..., B_NOPE:].reshape(B_KV_RANK, B_HEADS * B_V)
    kvb = _norm_matmul(p1, offs[4] // B_KV_RANK, B_KV_RANK, kv_norm_g,
                       jnp.concatenate([w_k, w_v], axis=1).astype(BF16))
    kb = kvb[:, :B_HEADS * LANES].reshape(t, B_HEADS, LANES)
    kb = jnp.concatenate([kb[..., :B_NOPE], jnp.broadcast_to(kpe, (t, B_HEADS, B_ROPE)), kb[..., qk_dim:]],
                         axis=-1).reshape(t, B_HEADS * LANES)
    vb = kvb[:, B_HEADS * LANES:]

    n_blk = A_HEADS // 2
    zero_cb = lambda h: 0
    ident = lambda h: h
    oa_l = _pair_attn(qa, ident, ka, zero_cb, va, zero_cb, n_blk, bsz, s_len, n_ctx, sink, window=True)
    oa_c = _pair_attn(qa, ident, ka, zero_cb, va, zero_cb, n_blk, bsz, s_len, n_ctx, sink, window=False)
    ob_l = _mla_attn(qb, kb, vb, bsz, s_len, n_ctx, latent=True)
    ob_c = _mla_attn(qb, kb, vb, bsz, s_len, n_ctx, latent=False)
    oa = jnp.concatenate([oa_l, oa_c], axis=0)
    ob = jnp.concatenate([ob_l, ob_c], axis=0)
    n_a = A_HEADS * HEAD_DIM
    w_oa = w_out[:n_a].reshape(A_HEADS, HEAD_DIM, d)[A_HEAD_ORDER, :, :].reshape(n_a, d)
    return [oa, ob], [w_oa.astype(BF16), w_out[n_a:].astype(BF16)]


def _odd_layer_attn(x, mod, seg_of_tile, norm1_g, w_in, rpb, w_out, bsz, s_len, n_ctx):
    d = x.shape[1]
    width = C_HEADS * HEAD_DIM
    w1 = jnp.concatenate([w_in[:, :width] * (HEAD_DIM ** -0.5), w_in[:, width:]], axis=1)
    qkv = _norm_matmul(x, 0, d, norm1_g, w1.astype(BF16), mod, 0, 1, seg_of_tile)
    n_pair = C_HEADS // 2
    o_l = _na_attn(qkv, rpb, bsz, s_len, n_ctx)
    o_c = _pair_attn(qkv, lambda h: h, qkv, lambda h: n_pair + h, qkv, lambda h: 2 * n_pair + h,
                     n_pair, bsz, s_len, n_ctx, None, window=False)
    return [jnp.concatenate([o_l, o_c], axis=0)], [w_out.astype(BF16)]


def kernel(x, c, ctx, c_ctx, ada_w, ada_b, norm1_g, norm2_g, ev_w_in, ev_sink, ev_q_norm_g, ev_w_uq,
           ev_kv_norm_g, ev_w_ukv, ev_w_out, od_w_in, od_rpb, od_w_out, router_w, router_b,
           exp_w1, exp_b1, exp_w2, exp_b2, final_g):
    bsz, s_len, d = x.shape
    n_ctx = ctx.shape[1]
    depth = ada_w.shape[0]
    n_lat = bsz * s_len
    assert bsz < MOD_ROWS and s_len % 512 == 0 and n_lat % n_ctx == 0

    def seg_of_tile(i, tm):
        return jnp.minimum(i * tm // s_len, bsz)

    c_rows = jnp.concatenate([c, c_ctx[None, :], jnp.zeros((MOD_ROWS - bsz - 1, d), F32)], axis=0)
    mods = _ada_modulation(c_rows, ada_w, ada_b)

    tpos = jnp.arange(s_len, dtype=jnp.int32)
    zeros_c = jnp.zeros((bsz * n_ctx,), jnp.int32)
    rows = jnp.concatenate([jnp.tile(tpos // GRID_W, bsz), zeros_c])
    cols = jnp.concatenate([jnp.tile(tpos % GRID_W, bsz), zeros_c])

    xs = jnp.concatenate([x.reshape(n_lat, d), ctx.reshape(bsz * n_ctx, d)], axis=0)
    for layer in range(depth):
        i = layer // 2
        mod = mods[layer].reshape(MOD_ROWS * 6, 1, d)
        if layer % 2 == 0:
            a_list, w_list = _even_layer_attn(xs, mod, seg_of_tile, norm1_g[layer], ev_w_in[i], ev_sink[i],
                                              ev_q_norm_g[i], ev_w_uq[i], ev_kv_norm_g[i], ev_w_ukv[i],
                                              ev_w_out[i], rows, cols, bsz, s_len, n_ctx)
        else:
            a_list, w_list = _odd_layer_attn(xs, mod, seg_of_tile, norm1_g[layer], od_w_in[i], od_rpb[i],
                                             od_w_out[i], bsz, s_len, n_ctx)
        xs, h2, top_idx, top_gate = _out_router(a_list, w_list, xs, mod, norm2_g[layer], router_w[layer],
                                                router_b[layer], seg_of_tile)
        y = _moe(h2, top_idx, top_gate, exp_w1[layer].astype(BF16), exp_b1[layer],
                 exp_w2[layer].astype(BF16), exp_b2[layer])
        gate2 = mods[layer].reshape(MOD_ROWS, 6, d)[:, 5, :]
        y_l = y[:n_lat].reshape(bsz, s_len, d) * gate2[:bsz, None, :]
        y_c = y[n_lat:] * gate2[bsz][None, :]
        xs = xs + jnp.concatenate([y_l.reshape(n_lat, d), y_c], axis=0)
    return _final_norm(xs, final_g, n_lat).reshape(bsz, s_len, d)
```

```python
import functools

import numpy as np
import jax
import jax.numpy as jnp
from jax import lax
from jax.experimental import pallas as pl
from jax.experimental.pallas import tpu as pltpu

GRID_W = 64
HEAD_DIM = 64
ROPE_BASE = 10000.0
NORM_EPS = 1e-6
NEG_INF = -1e30

A_HEADS = 8
A_KV_HEADS = 2
A_WINDOW = 128
B_HEADS = 8
B_NOPE = 64
B_ROPE = 32
B_V = 64
B_Q_RANK = 768
B_KV_RANK = 256
C_HEADS = 16
NA_ROWS = 8
NA_COLS = 16

N_EXPERTS = 32
TOP_K = 4
D_EXPERT = 1024
SWIGLU_LIMIT = 7.0
SWIGLU_ALPHA = 1.702

LANES = 128
MOD_ROWS = 8
VMEM_LIMIT = 48 * 1024 * 1024
MOE_VMEM_LIMIT = 56 * 1024 * 1024

BF16 = jnp.bfloat16
F32 = jnp.float32
NT_DIMS = (((1,), (1,)), ((), ()))
LOG2_E = 1.4426950408889634


def _cparams(*sem):
    return pltpu.CompilerParams(dimension_semantics=sem, vmem_limit_bytes=VMEM_LIMIT)


def _dot(a, b):
    return jnp.dot(a, b, preferred_element_type=F32)


def _dot_nt(a, b):
    return lax.dot_general(a, b, NT_DIMS, preferred_element_type=F32)


def _ada_kernel(c_ref, w_ref, b_ref, o_ref):
    c = c_ref[...]
    s = c / (1.0 + jnp.exp(-c))
    o_ref[0] = jnp.dot(s, w_ref[0], preferred_element_type=F32, precision=lax.Precision.HIGHEST) + b_ref[0]


def _ada_modulation(c_rows, ada_w, ada_b):
    depth, d, n = ada_w.shape
    tn = 1024
    return pl.pallas_call(
        _ada_kernel,
        grid=(depth, n // tn),
        in_specs=[
            pl.BlockSpec((MOD_ROWS, d), lambda l, j: (0, 0)),
            pl.BlockSpec((1, d, tn), lambda l, j: (l, 0, j)),
            pl.BlockSpec((1, 1, tn), lambda l, j: (l, 0, j)),
        ],
        out_specs=pl.BlockSpec((1, MOD_ROWS, tn), lambda l, j: (l, 0, j)),
        out_shape=jax.ShapeDtypeStruct((depth, MOD_ROWS, n), F32),
        compiler_params=_cparams("parallel", "parallel"),
        name="ada_modulation",
    )(c_rows, ada_w, ada_b.reshape(depth, 1, n))


def _norm_mm_kernel(*refs, modulate):
    if modulate:
        x_ref, g_ref, sh_ref, sc_ref, w_ref, o_ref = refs
    else:
        x_ref, g_ref, w_ref, o_ref = refs
    x = x_ref[...].astype(F32)
    ms = jnp.mean(x * x, axis=-1, keepdims=True)
    h = x * lax.rsqrt(ms + NORM_EPS) * g_ref[...]
    if modulate:
        h = h * (1.0 + sc_ref[0]) + sh_ref[0]
    o_ref[...] = _dot(h.astype(BF16), w_ref[...]).astype(o_ref.dtype)


def _row_tile(t):
    return 512 if t % 512 == 0 else 256


def _norm_matmul(x, col_block, kdim, g, w, mod=None, shift_idx=0, scale_idx=0, seg_of_tile=None):
    t = x.shape[0]
    n = w.shape[1]
    tm = _row_tile(t)
    in_specs = [pl.BlockSpec((tm, kdim), lambda i: (i, col_block)),
                pl.BlockSpec((1, kdim), lambda i: (0, 0))]
    args = [x, g.reshape(1, kdim).astype(F32)]
    if mod is not None:
        in_specs += [pl.BlockSpec((1, 1, kdim), lambda i: (seg_of_tile(i, tm) * 6 + shift_idx, 0, 0)),
                     pl.BlockSpec((1, 1, kdim), lambda i: (seg_of_tile(i, tm) * 6 + scale_idx, 0, 0))]
        args += [mod, mod]
    in_specs.append(pl.BlockSpec((kdim, n), lambda i: (0, 0)))
    args.append(w)
    return pl.pallas_call(
        functools.partial(_norm_mm_kernel, modulate=mod is not None),
        grid=(t // tm,),
        in_specs=in_specs,
        out_specs=pl.BlockSpec((tm, n), lambda i: (i, 0)),
        out_shape=jax.ShapeDtypeStruct((t, n), BF16),
        compiler_params=_cparams("parallel"),
        name="norm_matmul",
    )(*args)


def _out_router_kernel(*refs, n_a):
    a_refs = refs[:n_a]
    w_refs = refs[n_a:2 * n_a]
    x_ref, gate_ref, g2_ref, sh_ref, sc_ref, rwt_ref, rb_ref, tri_ref = refs[2 * n_a:2 * n_a + 8]
    xo_ref, h_ref, ti_ref, tg_ref, rk_ref, cnt_ref, cnt_scr = refs[2 * n_a + 8:]
    acc = _dot(a_refs[0][...], w_refs[0][...])
    for k in range(1, n_a):
        acc = acc + _dot(a_refs[k][...], w_refs[k][...])
    xn = x_ref[...] + gate_ref[0] * acc
    xo_ref[...] = xn
    ms = jnp.mean(xn * xn, axis=-1, keepdims=True)
    h = xn * lax.rsqrt(ms + NORM_EPS) * g2_ref[...]
    h = h * (1.0 + sc_ref[0]) + sh_ref[0]
    h_ref[...] = h.astype(BF16)
    logits = lax.dot_general(rwt_ref[...], h, NT_DIMS, preferred_element_type=F32,
                             precision=lax.Precision.HIGHEST) + rb_ref[...]
    eidx = lax.broadcasted_iota(jnp.int32, logits.shape, 0)
    vals, idxs, hots = [], [], []
    cur = logits
    for _ in range(TOP_K):
        m = jnp.max(cur, axis=0, keepdims=True)
        idx = jnp.min(jnp.where(cur == m, eidx, N_EXPERTS), axis=0, keepdims=True)
        hot = eidx == idx
        vals.append(m)
        idxs.append(idx)
        hots.append(hot)
        cur = jnp.where(hot, -jnp.inf, cur)
    es = [jnp.exp(v - vals[0]) for v in vals]
    den = es[0] + es[1] + es[2] + es[3]
    ti_ref[...] = jnp.concatenate(idxs, axis=0)
    tg_ref[...] = jnp.concatenate([e / den for e in es], axis=0)

    @pl.when(pl.program_id(0) == 0)
    def _():
        cnt_scr[...] = jnp.zeros_like(cnt_scr)

    chosen = jnp.where(hots[0] | hots[1] | hots[2] | hots[3], 1.0, 0.0)
    before = cnt_scr[...] + _dot(chosen.astype(BF16), tri_ref[...])
    rk_ref[...] = jnp.concatenate(
        [jnp.sum(jnp.where(hot, before, 0.0), axis=0, keepdims=True) for hot in hots], axis=0).astype(jnp.int32)
    total = cnt_scr[...] + jnp.sum(chosen, axis=1, keepdims=True)
    cnt_scr[...] = total
    cnt_ref[...] = jnp.broadcast_to(total, cnt_ref.shape)


def _out_router(a_list, w_list, x, mod, g2, router_w, router_b, seg_of_tile):
    t, d = x.shape
    tm = _row_tile(t)
    n_a = len(a_list)

    def mod_spec(idx):
        return pl.BlockSpec((1, 1, d), lambda i: (seg_of_tile(i, tm) * 6 + idx, 0, 0))

    in_specs = [pl.BlockSpec((tm, a.shape[1]), lambda i: (i, 0)) for a in a_list]
    in_specs += [pl.BlockSpec(w.shape, lambda i: (0, 0)) for w in w_list]
    in_specs += [pl.BlockSpec((tm, d), lambda i: (i, 0)), mod_spec(2),
                 pl.BlockSpec((1, d), lambda i: (0, 0)), mod_spec(3), mod_spec(4),
                 pl.BlockSpec((N_EXPERTS, d), lambda i: (0, 0)),
                 pl.BlockSpec((N_EXPERTS, 1), lambda i: (0, 0)),
                 pl.BlockSpec((tm, tm), lambda i: (0, 0))]
    strictly_upper = jnp.asarray(np.triu(np.ones((tm, tm), np.float32), k=1), BF16)
    return pl.pallas_call(
        functools.partial(_out_router_kernel, n_a=n_a),
        grid=(t // tm,),
        in_specs=in_specs,
        out_specs=[pl.BlockSpec((tm, d), lambda i: (i, 0)),
                   pl.BlockSpec((tm, d), lambda i: (i, 0)),
                   pl.BlockSpec((TOP_K, tm), lambda i: (0, i)),
                   pl.BlockSpec((TOP_K, tm), lambda i: (0, i)),
                   pl.BlockSpec((TOP_K, tm), lambda i: (0, i)),
                   pl.BlockSpec((N_EXPERTS, LANES), lambda i: (0, 0))],
        out_shape=[jax.ShapeDtypeStruct((t, d), F32),
                   jax.ShapeDtypeStruct((t, d), BF16),
                   jax.ShapeDtypeStruct((TOP_K, t), jnp.int32),
                   jax.ShapeDtypeStruct((TOP_K, t), F32),
                   jax.ShapeDtypeStruct((TOP_K, t), jnp.int32),
                   jax.ShapeDtypeStruct((N_EXPERTS, LANES), F32)],
        scratch_shapes=[pltpu.VMEM((N_EXPERTS, 1), F32)],
        compiler_params=_cparams("arbitrary"),
        name="out_router",
    )(*a_list, *w_list, x, mod, g2.reshape(1, d), mod, mod,
      router_w.T, router_b.reshape(N_EXPERTS, 1), strictly_upper)


def _half_mask(shape, j):
    lane = lax.broadcasted_iota(jnp.int32, shape, 1)
    return (lane >= HEAD_DIM * j) & (lane < HEAD_DIM * (j + 1))


def _softmax_pv(scores, values, sink=None):
    m = jnp.max(scores[0], axis=1, keepdims=True)
    for s in scores[1:]:
        m = jnp.maximum(m, jnp.max(s, axis=1, keepdims=True))
    if sink is not None:
        m = jnp.maximum(m, sink)
    den = None
    out = None
    for s, v in zip(scores, values):
        p = jnp.exp(s - m)
        ps = jnp.sum(p, axis=1, keepdims=True)
        den = ps if den is None else den + ps
        o = _dot(p.astype(BF16), v)
        out = o if out is None else out + o
    if sink is not None:
        den = den + jnp.exp(sink - m)
    return out / den


def _pair_attn_kernel(*refs, window, tq, s_len, has_sink, heads_per_group):
    if window:
        q_ref, kl_ref, vl_ref, kc_ref, vc_ref = refs[:5]
        rest = refs[5:]
    else:
        q_ref, kc_ref, vc_ref = refs[:3]
        rest = refs[3:]
    if has_sink:
        sink_ref, o_ref = rest
    else:
        (o_ref,) = rest
    blk = pl.program_id(1)
    q = q_ref[...]
    kc = kc_ref[...]
    vc = vc_ref[...]
    if window:
        i = pl.program_id(2)
        wl = tq + 2 * A_WINDOW
        start = pl.multiple_of(jnp.clip(i * tq - A_WINDOW, 0, s_len - wl), LANES)
        kw = kl_ref[pl.ds(start, wl), :]
        vw = vl_ref[pl.ds(start, wl), :]
        qpos = i * tq + lax.broadcasted_iota(jnp.int32, (tq, wl), 0)
        kpos = start + lax.broadcasted_iota(jnp.int32, (tq, wl), 1)
        band = jnp.abs(qpos - kpos) <= A_WINDOW
    outs = []
    for j in range(2):
        qj = jnp.where(_half_mask(q.shape, j), q, jnp.zeros_like(q))
        scores, values = [], []
        if window:
            scores.append(jnp.where(band, _dot_nt(qj, kw), NEG_INF))
            values.append(vw)
        scores.append(_dot_nt(qj, kc))
        values.append(vc)
        sink = sink_ref[j * heads_per_group + blk] if has_sink else None
        outs.append(_softmax_pv(scores, values, sink))
    o_ref[...] = jnp.where(_half_mask(outs[0].shape, 0), outs[0], outs[1]).astype(o_ref.dtype)


def _pair_attn(q_arr, q_cb, k_arr, k_cb, v_arr, v_cb, n_blk, bsz, s_len, n_ctx, sink, window):
    ctx_blk0 = bsz * s_len // n_ctx
    if window:
        tq = 512
        nq = s_len // tq
        q_spec = pl.BlockSpec((tq, LANES), lambda b, h, i: (b * nq + i, q_cb(h)))
        kv_specs = [pl.BlockSpec((s_len, LANES), lambda b, h, i: (b, k_cb(h))),
                    pl.BlockSpec((s_len, LANES), lambda b, h, i: (b, v_cb(h)))]
        args = [q_arr, k_arr, v_arr, k_arr, v_arr]
        out_rows = bsz * s_len
    else:
        tq = n_ctx
        nq = 1
        q_spec = pl.BlockSpec((tq, LANES), lambda b, h, i: (ctx_blk0 + b, q_cb(h)))
        kv_specs = []
        args = [q_arr, k_arr, v_arr]
        out_rows = bsz * n_ctx
    kv_specs += [pl.BlockSpec((n_ctx, LANES), lambda b, h, i: (ctx_blk0 + b, k_cb(h))),
                 pl.BlockSpec((n_ctx, LANES), lambda b, h, i: (ctx_blk0 + b, v_cb(h)))]
    in_specs = [q_spec] + kv_specs
    if sink is not None:
        in_specs.append(pl.BlockSpec(memory_space=pltpu.SMEM))
        args.append(sink.astype(F32))
    return pl.pallas_call(
        functools.partial(_pair_attn_kernel, window=window, tq=tq, s_len=s_len,
                          has_sink=sink is not None, heads_per_group=n_blk),
        grid=(bsz, n_blk, nq),
        in_specs=in_specs,
        out_specs=pl.BlockSpec((tq, LANES), lambda b, h, i: (b * nq + i, h)),
        out_shape=jax.ShapeDtypeStruct((out_rows, n_blk * LANES), BF16),
        compiler_params=_cparams("parallel", "parallel", "arbitrary"),
        name="pair_attn_window" if window else "pair_attn_ctx",
    )(*args)


def _mla_kernel(*refs, latent, tk, n_chunks, sub):
    if latent:
        q_ref, kl_ref, vl_ref, kc_ref, vc_ref, o_ref = refs
    else:
        q_ref, kc_ref, vc_ref, o_ref = refs
    streams = [(j, r) for j in range(2) for r in range(q_ref.shape[0] // sub)]

    def update(j, r, k, v, state):
        m, l, acc = state
        s = _dot_nt(q_ref[r * sub:(r + 1) * sub, LANES * j:LANES * (j + 1)], k)
        m_new = jnp.maximum(m, jnp.max(s, axis=1, keepdims=True))
        p = jnp.exp2(s - m_new)
        alpha = jnp.exp2(m - m_new)
        l_new = alpha * l + jnp.sum(p, axis=1, keepdims=True)
        acc_new = alpha * acc + _dot(p.astype(BF16), v)
        return m_new, l_new, acc_new

    init = (jnp.full((sub, 1), NEG_INF, F32), jnp.zeros((sub, 1), F32), jnp.zeros((sub, LANES), F32))
    vc = vc_ref[...]
    state = tuple(update(j, r, kc_ref[:, LANES * j:LANES * (j + 1)], vc, init) for j, r in streams)
    if latent:
        for c in range(n_chunks):
            v = vl_ref[c * tk:(c + 1) * tk, :]
            state = tuple(update(j, r, kl_ref[c * tk:(c + 1) * tk, LANES * j:LANES * (j + 1)], v, st)
                          for (j, r), st in zip(streams, state))
    for r in range(q_ref.shape[0] // sub):
        o0, o1 = [state[streams.index((j, r))] for j in range(2)]
        o0 = o0[2] / o0[1]
        o1 = o1[2] / o1[1]
        o_ref[r * sub:(r + 1) * sub, :] = jnp.where(_half_mask(o0.shape, 0), o0, o1).astype(o_ref.dtype)


def _mla_attn(q_arr, k_arr, v_arr, bsz, s_len, n_ctx, latent):
    n_pair = B_HEADS // 2
    ctx_blk0 = bsz * s_len // n_ctx
    kv_specs = [pl.BlockSpec((n_ctx, 2 * LANES), lambda b, h, i: (ctx_blk0 + b, h)),
                pl.BlockSpec((n_ctx, LANES), lambda b, h, i: (ctx_blk0 + b, h))]
    if latent:
        tq, tk = 512, min(2048, s_len)
        nq = s_len // tq
        q_spec = pl.BlockSpec((tq, 2 * LANES), lambda b, h, i: (b * nq + i, h))
        kv_specs = [pl.BlockSpec((s_len, 2 * LANES), lambda b, h, i: (b, h)),
                    pl.BlockSpec((s_len, LANES), lambda b, h, i: (b, h))] + kv_specs
        args = [q_arr, k_arr, v_arr, k_arr, v_arr]
        out_rows = bsz * s_len
    else:
        tq, tk = n_ctx, n_ctx
        nq = 1
        q_spec = pl.BlockSpec((tq, 2 * LANES), lambda b, h, i: (ctx_blk0 + b, h))
        args = [q_arr, k_arr, v_arr]
        out_rows = bsz * n_ctx
    return pl.pallas_call(
        functools.partial(_mla_kernel, latent=latent, tk=tk, n_chunks=s_len // tk, sub=min(tq, 512)),
        grid=(bsz, n_pair, nq),
        in_specs=[q_spec] + kv_specs,
        out_specs=pl.BlockSpec((tq, LANES), lambda b, h, i: (b * nq + i, h)),
        out_shape=jax.ShapeDtypeStruct((out_rows, n_pair * LANES), BF16),
        compiler_params=_cparams("parallel", "parallel", "arbitrary"),
        name="mla_latent" if latent else "mla_ctx",
    )(*args)


NA_Q_ROWS = 4
NA_K_ROWS = NA_Q_ROWS + NA_ROWS


def _na_kernel(pat_ref, start_ref, q_ref, kl_ref, vl_ref, kc_ref, vc_ref, bias_ref, o_ref):
    del pat_ref
    rb = pl.program_id(2)
    nk = NA_K_ROWS * GRID_W
    start = pl.multiple_of(start_ref[rb] * GRID_W, NA_Q_ROWS * GRID_W)
    kw = kl_ref[pl.ds(start, nk), :]
    vw = vl_ref[pl.ds(start, nk), :]
    kc = kc_ref[...]
    vc = vc_ref[...]
    q = q_ref[...]
    outs = []
    for j in range(2):
        qj = jnp.where(_half_mask(q.shape, j), q, jnp.zeros_like(q))
        sw = _dot_nt(qj, kw) + bias_ref[0, j]
        sc = _dot_nt(qj, kc)
        outs.append(_softmax_pv([sw, sc], [vw, vc]))
    o_ref[...] = jnp.where(_half_mask(outs[0].shape, 0), outs[0], outs[1]).astype(o_ref.dtype)


def _na_patterns(n_rows):
    kh = NA_ROWS
    n_rb = n_rows // NA_Q_ROWS
    starts, keys = [], []
    for rb in range(n_rb):
        r_a = rb * NA_Q_ROWS
        start = int(np.clip(r_a - NA_Q_ROWS, 0, n_rows - NA_K_ROWS))
        assert start % NA_Q_ROWS == 0
        rows = r_a + np.arange(NA_Q_ROWS)
        r0 = np.clip(rows - kh // 2, 0, n_rows - kh)
        assert start <= r0.min() and r0.max() + kh <= start + NA_K_ROWS
        starts.append(start)
        keys.append((r_a - start, tuple((r0 - start).tolist())))
    uniq = sorted(set(keys))
    pat = [uniq.index(k) for k in keys]
    return np.asarray(starts, np.int32), np.asarray(pat, np.int32), uniq


def _na_bias_table(rpb, uniq):
    n_dr, n_dc = 2 * NA_ROWS - 1, 2 * NA_COLS - 1
    i = np.arange(NA_Q_ROWS)[:, None]
    j = np.arange(NA_K_ROWS)[None, :]
    rsel = np.zeros((len(uniq), NA_Q_ROWS, NA_K_ROWS, n_dr), np.float32)
    rvalid = np.zeros((len(uniq), NA_Q_ROWS, NA_K_ROWS), bool)
    for p, (delta, r0_rel) in enumerate(uniq):
        r0_rel = np.asarray(r0_rel)[:, None]
        valid = (j >= r0_rel) & (j < r0_rel + NA_ROWS)
        dr = np.clip(j - (delta + i) + NA_ROWS - 1, 0, n_dr - 1)
        rsel[p] = np.eye(n_dr, dtype=np.float32)[dr] * valid[..., None]
        rvalid[p] = valid
    c = np.arange(GRID_W)[:, None]
    kc = np.arange(GRID_W)[None, :]
    c0 = np.clip(c - NA_COLS // 2, 0, GRID_W - NA_COLS)
    cvalid = (kc >= c0) & (kc < c0 + NA_COLS)
    dc = np.clip(kc - c + NA_COLS - 1, 0, n_dc - 1)
    csel = np.eye(n_dc, dtype=np.float32)[dc] * cvalid[..., None]
    hp = lax.Precision.HIGHEST
    tmp = jnp.einsum('hab,cqb->hacq', rpb.astype(F32), jnp.asarray(csel), precision=hp)
    bias = jnp.einsum('pija,hacq->phicjq', jnp.asarray(rsel), tmp, precision=hp)
    valid = rvalid[:, None, :, None, :, None] & cvalid[None, None, None, :, None, :]
    bias = jnp.where(jnp.asarray(valid), bias, NEG_INF)
    return bias.reshape(len(uniq), C_HEADS, NA_Q_ROWS * GRID_W, NA_K_ROWS * GRID_W)


def _na_attn(qkv, rpb, bsz, s_len, n_ctx):
    n_pair = C_HEADS // 2
    n_rows = s_len // GRID_W
    starts, pat, uniq = _na_patterns(n_rows)
    bias = _na_bias_table(rpb, uniq)
    n_rb = n_rows // NA_Q_ROWS
    tq = NA_Q_ROWS * GRID_W
    nk = NA_K_ROWS * GRID_W
    ctx_blk0 = bsz * s_len // n_ctx
    grid_spec = pltpu.PrefetchScalarGridSpec(
        num_scalar_prefetch=2,
        grid=(n_pair, bsz, n_rb),
        in_specs=[
            pl.BlockSpec((tq, LANES), lambda h, b, r, pat, st: (b * n_rb + r, h)),
            pl.BlockSpec((s_len, LANES), lambda h, b, r, pat, st: (b, n_pair + h)),
            pl.BlockSpec((s_len, LANES), lambda h, b, r, pat, st: (b, 2 * n_pair + h)),
            pl.BlockSpec((n_ctx, LANES), lambda h, b, r, pat, st: (ctx_blk0 + b, n_pair + h)),
            pl.BlockSpec((n_ctx, LANES), lambda h, b, r, pat, st: (ctx_blk0 + b, 2 * n_pair + h)),
            pl.BlockSpec((1, 2, tq, nk), lambda h, b, r, pat, st: (pat[r], h, 0, 0)),
        ],
        out_specs=pl.BlockSpec((tq, LANES), lambda h, b, r, pat, st: (b * n_rb + r, h)),
    )
    return pl.pallas_call(
        _na_kernel,
        grid_spec=grid_spec,
        out_shape=jax.ShapeDtypeStruct((bsz * s_len, n_pair * LANES), BF16),
        compiler_params=_cparams("parallel", "parallel", "arbitrary"),
        name="na_latent",
    )(jnp.asarray(pat), jnp.asarray(starts), qkv, qkv, qkv, qkv, qkv, bias)


MOE_BLOCK = 512
MOE_FC = 512


def _moe_kernel(be_ref, nu_ref, x_ref, w1_ref, b1_ref, w2_ref, b2_ref, o_ref, w1_bf, w2_bf):
    i = pl.program_id(0)

    @pl.when(i < nu_ref[0])
    def _():
        @pl.when((i == 0) | (be_ref[i] != be_ref[jnp.maximum(i - 1, 0)]))
        def _():
            w1_bf[...] = w1_ref[0].astype(BF16)
            w2_bf[...] = w2_ref[0].astype(BF16)

        x = x_ref[...]
        acc = None
        for c in range(D_EXPERT // MOE_FC):
            lo, hi = c * MOE_FC, (c + 1) * MOE_FC
            glu = _dot(x, w1_bf[:, lo:hi]) + b1_ref[0, :, lo:hi]
            lin = _dot(x, w1_bf[:, D_EXPERT + lo:D_EXPERT + hi]) + b1_ref[0, :, D_EXPERT + lo:D_EXPERT + hi]
            glu = jnp.minimum(glu, SWIGLU_LIMIT)
            lin = jnp.clip(lin, -SWIGLU_LIMIT, SWIGLU_LIMIT)
            act = glu * (1.0 / (1.0 + jnp.exp(-SWIGLU_ALPHA * glu))) * (lin + 1.0)
            y = _dot(act.astype(BF16), w2_bf[lo:hi, :])
            acc = y if acc is None else acc + y
        o_ref[...] = (acc + b2_ref[0]).astype(o_ref.dtype)


def _moe_experts(xs, blk_exp, n_used, w1, b1, w2, b2):
    n_slot, d = xs.shape
    n_blk = n_slot // MOE_BLOCK

    def blk(i, nu):
        return jnp.minimum(i, nu[0] - 1)

    grid_spec = pltpu.PrefetchScalarGridSpec(
        num_scalar_prefetch=2,
        grid=(n_blk,),
        in_specs=[
            pl.BlockSpec((MOE_BLOCK, d), lambda i, be, nu: (blk(i, nu), 0)),
            pl.BlockSpec((1, d, 2 * D_EXPERT), lambda i, be, nu: (be[blk(i, nu)], 0, 0)),
            pl.BlockSpec((1, 1, 2 * D_EXPERT), lambda i, be, nu: (be[blk(i, nu)], 0, 0)),
            pl.BlockSpec((1, D_EXPERT, d), lambda i, be, nu: (be[blk(i, nu)], 0, 0)),
            pl.BlockSpec((1, 1, d), lambda i, be, nu: (be[blk(i, nu)], 0, 0)),
        ],
        out_specs=pl.BlockSpec((MOE_BLOCK, d), lambda i, be, nu: (blk(i, nu), 0)),
        scratch_shapes=[pltpu.VMEM((d, 2 * D_EXPERT), BF16), pltpu.VMEM((D_EXPERT, d), BF16)],
    )
    return pl.pallas_call(
        _moe_kernel,
        grid_spec=grid_spec,
        out_shape=jax.ShapeDtypeStruct((n_slot, d), BF16),
        compiler_params=pltpu.CompilerParams(dimension_semantics=("arbitrary",),
                                             vmem_limit_bytes=MOE_VMEM_LIMIT),
        name="moe_experts",
    )(blk_exp, n_used, xs, w1, b1.reshape(N_EXPERTS, 1, 2 * D_EXPERT), w2, b2.reshape(N_EXPERTS, 1, d))


def _moe(h, top_idx, top_gate, rank, counts, w1, b1, w2, b2):
    t, d = h.shape
    n_asg = t * TOP_K
    padded = (counts + MOE_BLOCK - 1) // MOE_BLOCK * MOE_BLOCK
    pad_end = jnp.cumsum(padded)
    pad_start = pad_end - padded
    grp_start = jnp.cumsum(counts) - counts
    experts = jnp.arange(N_EXPERTS, dtype=jnp.int32)
    dest = rank + jnp.sum(jnp.where(top_idx[:, :, None] == experts, pad_start, 0), axis=-1)
    dest_flat = dest.reshape(n_asg)
    tok_flat = jnp.tile(jnp.arange(t, dtype=jnp.int32), TOP_K)
    _, tok_sorted = lax.sort((dest_flat, tok_flat), num_keys=1)
    n_blk = (n_asg + N_EXPERTS * (MOE_BLOCK - 1) + MOE_BLOCK - 1) // MOE_BLOCK
    blk_start = jnp.arange(n_blk, dtype=jnp.int32) * MOE_BLOCK
    blk_exp = jnp.minimum(jnp.sum((pad_end[None, :] <= blk_start[:, None]).astype(jnp.int32), axis=1),
                          N_EXPERTS - 1)
    n_used = (pad_end[-1:] // MOE_BLOCK).astype(jnp.int32)
    shift = jnp.repeat((grp_start - pad_start)[blk_exp], MOE_BLOCK)
    src = jnp.clip(jnp.arange(n_blk * MOE_BLOCK, dtype=jnp.int32) + shift, 0, n_asg - 1)
    ys = _moe_experts(h[tok_sorted[src]], blk_exp, n_used, w1, b1, w2, b2)
    yk = ys[dest_flat].reshape(TOP_K, t, d).astype(F32)
    return jnp.sum(yk * top_gate[:, :, None], axis=0)


def _final_norm_kernel(x_ref, g_ref, o_ref):
    x = x_ref[...]
    ms = jnp.mean(x * x, axis=-1, keepdims=True)
    o_ref[...] = x * lax.rsqrt(ms + NORM_EPS) * g_ref[...]


def _final_norm(x, g, rows):
    d = x.shape[1]
    tm = _row_tile(rows)
    return pl.pallas_call(
        _final_norm_kernel,
        grid=(rows // tm,),
        in_specs=[pl.BlockSpec((tm, d), lambda i: (i, 0)), pl.BlockSpec((1, d), lambda i: (0, 0))],
        out_specs=pl.BlockSpec((tm, d), lambda i: (i, 0)),
        out_shape=jax.ShapeDtypeStruct((rows, d), F32),
        compiler_params=_cparams("parallel"),
        name="final_norm",
    )(x, g.reshape(1, d))


def _rope_1d(x, pos):
    half = x.shape[-1] // 2
    inv_freq = ROPE_BASE ** (-(jnp.arange(half, dtype=F32) / half))
    ang = pos.astype(F32)[:, None] * inv_freq
    cos = jnp.cos(ang)[:, None, :]
    sin = jnp.sin(ang)[:, None, :]
    x1, x2 = x[..., :half], x[..., half:]
    return jnp.concatenate([x1 * cos - x2 * sin, x2 * cos + x1 * sin], axis=-1)


def _rope2d(x, rows, cols):
    h = x.shape[-1] // 2
    xf = x.astype(F32)
    return jnp.concatenate([_rope_1d(xf[..., :h], rows), _rope_1d(xf[..., h:], cols)], axis=-1).astype(x.dtype)


def _pad_cols(w, n):
    return jnp.pad(w, ((0, 0), (0, n - w.shape[1])))


A_HEAD_ORDER = tuple(h for blk in range(A_HEADS // 2) for h in (blk, blk + A_HEADS // 2))


def _even_layer_attn(x, mod, seg_of_tile, norm1_g, w_in, sink, q_norm_g, w_uq, kv_norm_g, w_ukv, w_out,
                     rows, cols, bsz, s_len, n_ctx):
    d = x.shape[1]
    t = x.shape[0]
    n_lat = bsz * s_len
    sizes = (A_HEADS * HEAD_DIM, A_KV_HEADS * HEAD_DIM, A_KV_HEADS * HEAD_DIM, B_Q_RANK, B_KV_RANK, B_ROPE)
    offs = np.cumsum((0,) + sizes)
    w_qa = w_in[:, offs[0]:offs[1]].reshape(d, A_HEADS, HEAD_DIM)[:, A_HEAD_ORDER, :].reshape(d, -1) * (HEAD_DIM ** -0.5)
    w1 = jnp.concatenate([w_qa, w_in[:, offs[1]:]], axis=1)
    n1 = -(-w1.shape[1] // LANES) * LANES
    p1 = _norm_matmul(x, 0, d, norm1_g, _pad_cols(w1, n1).astype(BF16), mod, 0, 1, seg_of_tile)

    qa = _rope2d(p1[:, offs[0]:offs[1]].reshape(t, A_HEADS, HEAD_DIM), rows, cols).reshape(t, -1)
    ka = _rope2d(p1[:, offs[1]:offs[2]].reshape(t, A_KV_HEADS, HEAD_DIM), rows, cols).reshape(t, -1)
    va = p1[:, offs[2]:offs[3]]
    kpe = _rope2d(p1[:, offs[5]:offs[6]].reshape(t, 1, B_ROPE), rows, cols)

    qk_dim = B_NOPE + B_ROPE
    w_q = jnp.pad(w_uq.reshape(B_Q_RANK, B_HEADS, qk_dim) * (qk_dim ** -0.5 * LOG2_E),
                  ((0, 0), (0, 0), (0, LANES - qk_dim))).reshape(B_Q_RANK, B_HEADS * LANES)
    qb = _norm_matmul(p1, offs[3] // B_Q_RANK, B_Q_RANK, q_norm_g, w_q.astype(BF16)).reshape(t, B_HEADS, LANES)
    qb = jnp.concatenate([qb[..., :B_NOPE], _rope2d(qb[..., B_NOPE:qk_dim], rows, cols), qb[..., qk_dim:]],
                         axis=-1).reshape(t, B_HEADS * LANES)
    w_kv = w_ukv.reshape(B_KV_RANK, B_HEADS, B_NOPE + B_V)
    w_k = jnp.pad(w_kv[..., :B_NOPE], ((0, 0), (0, 0), (0, LANES - B_NOPE))).reshape(B_KV_RANK, B_HEADS * LANES)
    w_v = w_kv[..., B_NOPE:].reshape(B_KV_RANK, B_HEADS * B_V)
    kvb = _norm_matmul(p1, offs[4] // B_KV_RANK, B_KV_RANK, kv_norm_g,
                       jnp.concatenate([w_k, w_v], axis=1).astype(BF16))
    kb = kvb[:, :B_HEADS * LANES].reshape(t, B_HEADS, LANES)
    kb = jnp.concatenate([kb[..., :B_NOPE], jnp.broadcast_to(kpe, (t, B_HEADS, B_ROPE)), kb[..., qk_dim:]],
                         axis=-1).reshape(t, B_HEADS * LANES)
    vb = kvb[:, B_HEADS * LANES:]

    n_blk = A_HEADS // 2
    zero_cb = lambda h: 0
    ident = lambda h: h
    oa_l = _pair_attn(qa, ident, ka, zero_cb, va, zero_cb, n_blk, bsz, s_len, n_ctx, sink, window=True)
    oa_c = _pair_attn(qa, ident, ka, zero_cb, va, zero_cb, n_blk, bsz, s_len, n_ctx, sink, window=False)
    ob_l = _mla_attn(qb, kb, vb, bsz, s_len, n_ctx, latent=True)
    ob_c = _mla_attn(qb, kb, vb, bsz, s_len, n_ctx, latent=False)
    oa = jnp.concatenate([oa_l, oa_c], axis=0)
    ob = jnp.concatenate([ob_l, ob_c], axis=0)
    n_a = A_HEADS * HEAD_DIM
    w_oa = w_out[:n_a].reshape(A_HEADS, HEAD_DIM, d)[A_HEAD_ORDER, :, :].reshape(n_a, d)
    return [oa, ob], [w_oa.astype(BF16), w_out[n_a:].astype(BF16)]


def _odd_layer_attn(x, mod, seg_of_tile, norm1_g, w_in, rpb, w_out, bsz, s_len, n_ctx):
    d = x.shape[1]
    width = C_HEADS * HEAD_DIM
    w1 = jnp.concatenate([w_in[:, :width] * (HEAD_DIM ** -0.5), w_in[:, width:]], axis=1)
    qkv = _norm_matmul(x, 0, d, norm1_g, w1.astype(BF16), mod, 0, 1, seg_of_tile)
    n_pair = C_HEADS // 2
    o_l = _na_attn(qkv, rpb, bsz, s_len, n_ctx)
    o_c = _pair_attn(qkv, lambda h: h, qkv, lambda h: n_pair + h, qkv, lambda h: 2 * n_pair + h,
                     n_pair, bsz, s_len, n_ctx, None, window=False)
    return [jnp.concatenate([o_l, o_c], axis=0)], [w_out.astype(BF16)]


def kernel(x, c, ctx, c_ctx, ada_w, ada_b, norm1_g, norm2_g, ev_w_in, ev_sink, ev_q_norm_g, ev_w_uq,
           ev_kv_norm_g, ev_w_ukv, ev_w_out, od_w_in, od_rpb, od_w_out, router_w, router_b,
           exp_w1, exp_b1, exp_w2, exp_b2, final_g):
    bsz, s_len, d = x.shape
    n_ctx = ctx.shape[1]
    depth = ada_w.shape[0]
    n_lat = bsz * s_len
    assert bsz < MOD_ROWS and s_len % 512 == 0 and n_lat % n_ctx == 0

    def seg_of_tile(i, tm):
        return jnp.minimum(i * tm // s_len, bsz)

    c_rows = jnp.concatenate([c, c_ctx[None, :], jnp.zeros((MOD_ROWS - bsz - 1, d), F32)], axis=0)
    mods = _ada_modulation(c_rows, ada_w, ada_b)

    tpos = jnp.arange(s_len, dtype=jnp.int32)
    zeros_c = jnp.zeros((bsz * n_ctx,), jnp.int32)
    rows = jnp.concatenate([jnp.tile(tpos // GRID_W, bsz), zeros_c])
    cols = jnp.concatenate([jnp.tile(tpos % GRID_W, bsz), zeros_c])

    xs = jnp.concatenate([x.reshape(n_lat, d), ctx.reshape(bsz * n_ctx, d)], axis=0)
    for layer in range(depth):
        i = layer // 2
        mod = mods[layer].reshape(MOD_ROWS * 6, 1, d)
        if layer % 2 == 0:
            a_list, w_list = _even_layer_attn(xs, mod, seg_of_tile, norm1_g[layer], ev_w_in[i], ev_sink[i],
                                              ev_q_norm_g[i], ev_w_uq[i], ev_kv_norm_g[i], ev_w_ukv[i],
                                              ev_w_out[i], rows, cols, bsz, s_len, n_ctx)
        else:
            a_list, w_list = _odd_layer_attn(xs, mod, seg_of_tile, norm1_g[layer], od_w_in[i], od_rpb[i],
                                             od_w_out[i], bsz, s_len, n_ctx)
        xs, h2, top_idx, top_gate, rank, counts = _out_router(a_list, w_list, xs, mod, norm2_g[layer],
                                                              router_w[layer], router_b[layer], seg_of_tile)
        y = _moe(h2, top_idx, top_gate, rank, counts[:, 0].astype(jnp.int32), exp_w1[layer], exp_b1[layer],
                 exp_w2[layer], exp_b2[layer])
        gate2 = mods[layer].reshape(MOD_ROWS, 6, d)[:, 5, :]
        y_l = y[:n_lat].reshape(bsz, s_len, d) * gate2[:bsz, None, :]
        y_c = y[n_lat:] * gate2[bsz][None, :]
        xs = xs + jnp.concatenate([y_l.reshape(n_lat, d), y_c], axis=0)
    return _final_norm(xs, final_g, n_lat).reshape(bsz, s_len, d)
```

```python
import functools

import numpy as np
import jax
import jax.numpy as jnp
from jax import lax
from jax.experimental import pallas as pl
from jax.experimental.pallas import tpu as pltpu

GRID_W = 64
HEAD_DIM = 64
ROPE_BASE = 10000.0
NORM_EPS = 1e-6
NEG_INF = -1e30

A_HEADS = 8
A_KV_HEADS = 2
A_WINDOW = 128
B_HEADS = 8
B_NOPE = 64
B_ROPE = 32
B_V = 64
B_Q_RANK = 768
B_KV_RANK = 256
C_HEADS = 16
NA_ROWS = 8
NA_COLS = 16

N_EXPERTS = 32
TOP_K = 4
D_EXPERT = 1024
SWIGLU_LIMIT = 7.0
SWIGLU_ALPHA = 1.702

LANES = 128
MOD_ROWS = 8
VMEM_LIMIT = 48 * 1024 * 1024
MOE_VMEM_LIMIT = 56 * 1024 * 1024

BF16 = jnp.bfloat16
F32 = jnp.float32
NT_DIMS = (((1,), (1,)), ((), ()))
LOG2_E = 1.4426950408889634


def _cparams(*sem):
    return pltpu.CompilerParams(dimension_semantics=sem, vmem_limit_bytes=VMEM_LIMIT)


def _dot(a, b):
    return jnp.dot(a, b, preferred_element_type=F32)


def _dot_nt(a, b):
    return lax.dot_general(a, b, NT_DIMS, preferred_element_type=F32)


def _ada_kernel(c_ref, w_ref, b_ref, o_ref):
    c = c_ref[...]
    s = c / (1.0 + jnp.exp(-c))
    o_ref[0] = jnp.dot(s, w_ref[0], preferred_element_type=F32, precision=lax.Precision.HIGHEST) + b_ref[0]


def _ada_modulation(c_rows, ada_w, ada_b):
    depth, d, n = ada_w.shape
    tn = 1024
    return pl.pallas_call(
        _ada_kernel,
        grid=(depth, n // tn),
        in_specs=[
            pl.BlockSpec((MOD_ROWS, d), lambda l, j: (0, 0)),
            pl.BlockSpec((1, d, tn), lambda l, j: (l, 0, j)),
            pl.BlockSpec((1, 1, tn), lambda l, j: (l, 0, j)),
        ],
        out_specs=pl.BlockSpec((1, MOD_ROWS, tn), lambda l, j: (l, 0, j)),
        out_shape=jax.ShapeDtypeStruct((depth, MOD_ROWS, n), F32),
        compiler_params=_cparams("parallel", "parallel"),
        name="ada_modulation",
    )(c_rows, ada_w, ada_b.reshape(depth, 1, n))


def _norm_mm_kernel(*refs, modulate):
    if modulate:
        x_ref, g_ref, sh_ref, sc_ref, w_ref, o_ref = refs
    else:
        x_ref, g_ref, w_ref, o_ref = refs
    x = x_ref[...].astype(F32)
    ms = jnp.mean(x * x, axis=-1, keepdims=True)
    h = x * lax.rsqrt(ms + NORM_EPS) * g_ref[...]
    if modulate:
        h = h * (1.0 + sc_ref[0]) + sh_ref[0]
    o_ref[...] = _dot(h.astype(BF16), w_ref[...]).astype(o_ref.dtype)


def _row_tile(t):
    return 512 if t % 512 == 0 else 256


def _norm_matmul(x, col_block, kdim, g, w, mod=None, shift_idx=0, scale_idx=0, seg_of_tile=None):
    t = x.shape[0]
    n = w.shape[1]
    tm = _row_tile(t)
    in_specs = [pl.BlockSpec((tm, kdim), lambda i: (i, col_block)),
                pl.BlockSpec((1, kdim), lambda i: (0, 0))]
    args = [x, g.reshape(1, kdim).astype(F32)]
    if mod is not None:
        in_specs += [pl.BlockSpec((1, 1, kdim), lambda i: (seg_of_tile(i, tm) * 6 + shift_idx, 0, 0)),
                     pl.BlockSpec((1, 1, kdim), lambda i: (seg_of_tile(i, tm) * 6 + scale_idx, 0, 0))]
        args += [mod, mod]
    in_specs.append(pl.BlockSpec((kdim, n), lambda i: (0, 0)))
    args.append(w)
    return pl.pallas_call(
        functools.partial(_norm_mm_kernel, modulate=mod is not None),
        grid=(t // tm,),
        in_specs=in_specs,
        out_specs=pl.BlockSpec((tm, n), lambda i: (i, 0)),
        out_shape=jax.ShapeDtypeStruct((t, n), BF16),
        compiler_params=_cparams("parallel"),
        name="norm_matmul",
    )(*args)


EV_QA, EV_KA, EV_VA = 0, 512, 640
EV_QB, EV_KB, EV_VB, EV_OUT = 768, 1792, 2816, 3328
P_QA, P_QA_ROT, P_KA, P_KA_ROT, P_VA, P_CQ, P_CKV, P_KPE, P_KPE_ROT, P_END = (
    0, 512, 1024, 1152, 1280, 1408, 2176, 2432, 2560, 2688)
TAB_COS64, TAB_SIN64, TAB_COSQ, TAB_SINQ, TAB_COSK, TAB_SINK, TAB_END = 0, 128, 256, 384, 512, 640, 768


def _rms(x, g):
    return x * lax.rsqrt(jnp.mean(x * x, axis=-1, keepdims=True) + NORM_EPS) * g


def _even_proj_kernel(x_ref, g_ref, sh_ref, sc_ref, w1_ref, tab_ref, gq_ref, wq_ref, gkv_ref, wk_ref, o_ref):
    h = _rms(x_ref[...], g_ref[...]) * (1.0 + sc_ref[0]) + sh_ref[0]
    p = _dot(h.astype(BF16), w1_ref[...])
    cos64, sin64 = tab_ref[:, TAB_COS64:TAB_SIN64], tab_ref[:, TAB_SIN64:TAB_COSQ]
    for b in range(A_HEADS // 2 + 1):
        lo = P_QA + LANES * b if b < A_HEADS // 2 else P_KA
        rot = P_QA_ROT + LANES * b if b < A_HEADS // 2 else P_KA_ROT
        o_ref[:, EV_QA + LANES * b:EV_QA + LANES * (b + 1)] = (
            p[:, lo:lo + LANES] * cos64 + p[:, rot:rot + LANES] * sin64).astype(BF16)
    o_ref[:, EV_VA:EV_QB] = p[:, P_VA:P_CQ].astype(BF16)

    cq = _rms(p[:, P_CQ:P_CKV], gq_ref[...]).astype(BF16)
    q2 = _dot(cq, wq_ref[...])
    cosq, sinq = tab_ref[:, TAB_COSQ:TAB_SINQ], tab_ref[:, TAB_SINQ:TAB_COSK]
    n_q = B_HEADS * LANES
    for hd in range(B_HEADS):
        lo = LANES * hd
        o_ref[:, EV_QB + lo:EV_QB + lo + LANES] = (
            q2[:, lo:lo + LANES] * cosq + q2[:, n_q + lo:n_q + lo + LANES] * sinq).astype(BF16)

    ckv = _rms(p[:, P_CKV:P_KPE], gkv_ref[...]).astype(BF16)
    kpe = (p[:, P_KPE:P_KPE_ROT] * tab_ref[:, TAB_COSK:TAB_SINK]
           + p[:, P_KPE_ROT:P_END] * tab_ref[:, TAB_SINK:TAB_END]).astype(BF16)
    o_ref[:, EV_KB:EV_OUT] = _dot(jnp.concatenate([ckv, kpe], axis=1), wk_ref[...]).astype(BF16)


def _rot_cols(w, half):
    g = w.reshape(w.shape[:-1] + (w.shape[-1] // (2 * half), 2, half))
    return jnp.stack([-g[..., 1, :], g[..., 0, :]], axis=-2).reshape(w.shape)


def _rope_tables(s_len, tm):
    t = jnp.arange(s_len, dtype=jnp.int32)
    pos = ((t // GRID_W).astype(F32)[:, None], (t % GRID_W).astype(F32)[:, None])

    def pattern(n):
        half = n // 4
        inv_freq = ROPE_BASE ** (-(jnp.arange(half, dtype=F32) / half))
        ang = jnp.concatenate([pos[0] * inv_freq, pos[0] * inv_freq, pos[1] * inv_freq, pos[1] * inv_freq], axis=1)
        return jnp.cos(ang), jnp.sin(ang)

    c64, s64 = pattern(HEAD_DIM)
    c32, s32 = pattern(B_ROPE)
    one = lambda n: jnp.ones((s_len, n), F32)
    zero = lambda n: jnp.zeros((s_len, n), F32)
    rest = LANES - B_NOPE - B_ROPE
    tab = jnp.concatenate([
        c64, c64, s64, s64,
        one(B_NOPE), c32, one(rest), zero(B_NOPE), s32, zero(rest),
        c32, one(LANES - B_ROPE), s32, zero(LANES - B_ROPE)], axis=1)
    ident = jnp.concatenate([jnp.ones((tm, LANES), F32), jnp.zeros((tm, LANES), F32)] * 3, axis=1)
    return jnp.concatenate([tab, ident], axis=0)


def _even_proj(x, mod, seg_of_tile, norm1_g, w_in, q_norm_g, w_uq, kv_norm_g, w_ukv, tables, bsz, s_len):
    t, d = x.shape
    tm = _row_tile(t)
    sizes = (A_HEADS * HEAD_DIM, A_KV_HEADS * HEAD_DIM, A_KV_HEADS * HEAD_DIM, B_Q_RANK, B_KV_RANK, B_ROPE)
    offs = np.cumsum((0,) + sizes)
    part = [w_in[:, offs[k]:offs[k + 1]] for k in range(6)]
    w_qa = part[0].reshape(d, A_HEADS, HEAD_DIM)[:, A_HEAD_ORDER, :].reshape(d, -1) * (HEAD_DIM ** -0.5)
    pad_blk = lambda w: _pad_cols(w, LANES)
    w1 = jnp.concatenate([w_qa, _rot_cols(w_qa, HEAD_DIM // 4), part[1], _rot_cols(part[1], HEAD_DIM // 4), part[2],
                          part[3], part[4], pad_blk(part[5]), pad_blk(_rot_cols(part[5], B_ROPE // 4))],
                         axis=1).astype(BF16)
    assert w1.shape[1] == P_END
    qk_dim = B_NOPE + B_ROPE
    rest = LANES - qk_dim
    wq = w_uq.reshape(B_Q_RANK, B_HEADS, qk_dim) * (qk_dim ** -0.5 * LOG2_E)
    wq_plain = jnp.pad(wq, ((0, 0), (0, 0), (0, rest)))
    wq_rot = jnp.pad(_rot_cols(wq[..., B_NOPE:], B_ROPE // 4), ((0, 0), (0, 0), (B_NOPE, rest)))
    wq2 = jnp.concatenate([wq_plain.reshape(B_Q_RANK, -1), wq_rot.reshape(B_Q_RANK, -1)], axis=1).astype(BF16)
    wkv = w_ukv.reshape(B_KV_RANK, B_HEADS, B_NOPE + B_V)
    wk = jnp.pad(wkv[..., :B_NOPE], ((0, 0), (0, 0), (0, LANES - B_NOPE))).reshape(B_KV_RANK, -1)
    place = jnp.pad(jnp.eye(B_ROPE, dtype=F32), ((0, LANES - B_ROPE), (B_NOPE, rest)))
    wk2 = jnp.concatenate([
        jnp.concatenate([wk, wkv[..., B_NOPE:].reshape(B_KV_RANK, -1)], axis=1),
        jnp.concatenate([jnp.tile(place, (1, B_HEADS)), jnp.zeros((LANES, B_HEADS * B_V), F32)], axis=1)],
        axis=0).astype(BF16)
    n_lat_tiles = bsz * s_len // tm
    per_batch = s_len // tm

    def tab_map(i):
        return (jnp.where(i < n_lat_tiles, i % per_batch, per_batch), 0)

    const = lambda i: (0, 0)
    return pl.pallas_call(
        _even_proj_kernel,
        grid=(t // tm,),
        in_specs=[
            pl.BlockSpec((tm, d), lambda i: (i, 0)),
            pl.BlockSpec((1, d), const),
            pl.BlockSpec((1, 1, d), lambda i: (seg_of_tile(i, tm) * 6 + 0, 0, 0)),
            pl.BlockSpec((1, 1, d), lambda i: (seg_of_tile(i, tm) * 6 + 1, 0, 0)),
            pl.BlockSpec(w1.shape, const),
            pl.BlockSpec((tm, TAB_END), tab_map),
            pl.BlockSpec((1, B_Q_RANK), const),
            pl.BlockSpec(wq2.shape, const),
            pl.BlockSpec((1, B_KV_RANK), const),
            pl.BlockSpec(wk2.shape, const),
        ],
        out_specs=pl.BlockSpec((tm, EV_OUT), lambda i: (i, 0)),
        out_shape=jax.ShapeDtypeStruct((t, EV_OUT), BF16),
        compiler_params=pltpu.CompilerParams(dimension_semantics=("parallel",), vmem_limit_bytes=MOE_VMEM_LIMIT),
        name="even_proj",
    )(x, norm1_g.reshape(1, d), mod, mod, w1, tables, q_norm_g.reshape(1, -1), wq2,
      kv_norm_g.reshape(1, -1), wk2)


def _out_router_kernel(*refs, n_a, n_lat_tiles):
    al_refs = refs[:n_a]
    ac_refs = refs[n_a:2 * n_a]
    w_refs = refs[2 * n_a:3 * n_a]
    x_ref, gate_ref, g2_ref, sh_ref, sc_ref, rwt_ref, rb_ref, tri_ref = refs[3 * n_a:3 * n_a + 8]
    xo_ref, h_ref, ti_ref, tg_ref, rk_ref, cnt_ref, cnt_scr = refs[3 * n_a + 8:]
    is_latent = pl.program_id(0) < n_lat_tiles
    acc = None
    for k in range(n_a):
        a = jnp.where(is_latent, al_refs[k][...], ac_refs[k][...])
        part = _dot(a, w_refs[k][...])
        acc = part if acc is None else acc + part
    xn = x_ref[...] + gate_ref[0] * acc
    xo_ref[...] = xn
    ms = jnp.mean(xn * xn, axis=-1, keepdims=True)
    h = xn * lax.rsqrt(ms + NORM_EPS) * g2_ref[...]
    h = h * (1.0 + sc_ref[0]) + sh_ref[0]
    h_ref[...] = h.astype(BF16)
    logits = lax.dot_general(rwt_ref[...], h, NT_DIMS, preferred_element_type=F32,
                             precision=lax.Precision.HIGHEST) + rb_ref[...]
    eidx = lax.broadcasted_iota(jnp.int32, logits.shape, 0)
    vals, idxs, hots = [], [], []
    cur = logits
    for _ in range(TOP_K):
        m = jnp.max(cur, axis=0, keepdims=True)
        idx = jnp.min(jnp.where(cur == m, eidx, N_EXPERTS), axis=0, keepdims=True)
        hot = eidx == idx
        vals.append(m)
        idxs.append(idx)
        hots.append(hot)
        cur = jnp.where(hot, -jnp.inf, cur)
    es = [jnp.exp(v - vals[0]) for v in vals]
    den = es[0] + es[1] + es[2] + es[3]
    ti_ref[...] = jnp.concatenate(idxs, axis=0)
    tg_ref[...] = jnp.concatenate([e / den for e in es], axis=0)

    @pl.when(pl.program_id(0) == 0)
    def _():
        cnt_scr[...] = jnp.zeros_like(cnt_scr)

    chosen = jnp.where(hots[0] | hots[1] | hots[2] | hots[3], 1.0, 0.0)
    before = cnt_scr[...] + _dot(chosen.astype(BF16), tri_ref[...])
    rk_ref[...] = jnp.concatenate(
        [jnp.sum(jnp.where(hot, before, 0.0), axis=0, keepdims=True) for hot in hots], axis=0).astype(jnp.int32)
    total = cnt_scr[...] + jnp.sum(chosen, axis=1, keepdims=True)
    cnt_scr[...] = total
    cnt_ref[...] = jnp.broadcast_to(total, cnt_ref.shape)


def _out_router(a_list, w_list, x, mod, g2, router_w, router_b, seg_of_tile):
    t, d = x.shape
    tm = _row_tile(t)
    n_a = len(a_list)
    n_lat_tiles = a_list[0][0].shape[0] // tm

    def mod_spec(idx):
        return pl.BlockSpec((1, 1, d), lambda i: (seg_of_tile(i, tm) * 6 + idx, 0, 0))

    in_specs = [pl.BlockSpec((tm, al.shape[1]), lambda i: (jnp.minimum(i, n_lat_tiles - 1), 0)) for al, _ in a_list]
    in_specs += [pl.BlockSpec((tm, ac.shape[1]), lambda i: (jnp.maximum(i - n_lat_tiles, 0), 0)) for _, ac in a_list]
    in_specs += [pl.BlockSpec(w.shape, lambda i: (0, 0)) for w in w_list]
    in_specs += [pl.BlockSpec((tm, d), lambda i: (i, 0)), mod_spec(2),
                 pl.BlockSpec((1, d), lambda i: (0, 0)), mod_spec(3), mod_spec(4),
                 pl.BlockSpec((N_EXPERTS, d), lambda i: (0, 0)),
                 pl.BlockSpec((N_EXPERTS, 1), lambda i: (0, 0)),
                 pl.BlockSpec((tm, tm), lambda i: (0, 0))]
    strictly_upper = jnp.asarray(np.triu(np.ones((tm, tm), np.float32), k=1), BF16)
    return pl.pallas_call(
        functools.partial(_out_router_kernel, n_a=n_a, n_lat_tiles=n_lat_tiles),
        grid=(t // tm,),
        in_specs=in_specs,
        out_specs=[pl.BlockSpec((tm, d), lambda i: (i, 0)),
                   pl.BlockSpec((tm, d), lambda i: (i, 0)),
                   pl.BlockSpec((TOP_K, tm), lambda i: (0, i)),
                   pl.BlockSpec((TOP_K, tm), lambda i: (0, i)),
                   pl.BlockSpec((TOP_K, tm), lambda i: (0, i)),
                   pl.BlockSpec((N_EXPERTS, LANES), lambda i: (0, 0))],
        out_shape=[jax.ShapeDtypeStruct((t, d), F32),
                   jax.ShapeDtypeStruct((t, d), BF16),
                   jax.ShapeDtypeStruct((TOP_K, t), jnp.int32),
                   jax.ShapeDtypeStruct((TOP_K, t), F32),
                   jax.ShapeDtypeStruct((TOP_K, t), jnp.int32),
                   jax.ShapeDtypeStruct((N_EXPERTS, LANES), F32)],
        scratch_shapes=[pltpu.VMEM((N_EXPERTS, 1), F32)],
        compiler_params=_cparams("arbitrary"),
        name="out_router",
    )(*[al for al, _ in a_list], *[ac for _, ac in a_list], *w_list, x, mod, g2.reshape(1, d), mod, mod,
      router_w.T, router_b.reshape(N_EXPERTS, 1), strictly_upper)


def _half_mask(shape, j):
    lane = lax.broadcasted_iota(jnp.int32, shape, 1)
    return (lane >= HEAD_DIM * j) & (lane < HEAD_DIM * (j + 1))


def _softmax_pv(scores, values, sink=None):
    m = jnp.max(scores[0], axis=1, keepdims=True)
    for s in scores[1:]:
        m = jnp.maximum(m, jnp.max(s, axis=1, keepdims=True))
    if sink is not None:
        m = jnp.maximum(m, sink)
    den = None
    out = None
    for s, v in zip(scores, values):
        p = jnp.exp(s - m)
        ps = jnp.sum(p, axis=1, keepdims=True)
        den = ps if den is None else den + ps
        o = _dot(p.astype(BF16), v)
        out = o if out is None else out + o
    if sink is not None:
        den = den + jnp.exp(sink - m)
    return out / den


def _pair_attn_kernel(*refs, window, tq, s_len, has_sink, heads_per_group):
    if window:
        q_ref, kl_ref, vl_ref, kc_ref, vc_ref = refs[:5]
        rest = refs[5:]
    else:
        q_ref, kc_ref, vc_ref = refs[:3]
        rest = refs[3:]
    if has_sink:
        sink_ref, o_ref = rest
    else:
        (o_ref,) = rest
    blk = pl.program_id(1)
    q = q_ref[...]
    kc = kc_ref[...]
    vc = vc_ref[...]
    if window:
        i = pl.program_id(2)
        wl = tq + 2 * A_WINDOW
        start = pl.multiple_of(jnp.clip(i * tq - A_WINDOW, 0, s_len - wl), LANES)
        kw = kl_ref[pl.ds(start, wl), :]
        vw = vl_ref[pl.ds(start, wl), :]
        qpos = i * tq + lax.broadcasted_iota(jnp.int32, (tq, wl), 0)
        kpos = start + lax.broadcasted_iota(jnp.int32, (tq, wl), 1)
        band = jnp.abs(qpos - kpos) <= A_WINDOW
    outs = []
    for j in range(2):
        qj = jnp.where(_half_mask(q.shape, j), q, jnp.zeros_like(q))
        scores, values = [], []
        if window:
            scores.append(jnp.where(band, _dot_nt(qj, kw), NEG_INF))
            values.append(vw)
        scores.append(_dot_nt(qj, kc))
        values.append(vc)
        sink = sink_ref[j * heads_per_group + blk] if has_sink else None
        outs.append(_softmax_pv(scores, values, sink))
    o_ref[...] = jnp.where(_half_mask(outs[0].shape, 0), outs[0], outs[1]).astype(o_ref.dtype)


def _pair_attn(q_arr, q_cb, k_arr, k_cb, v_arr, v_cb, n_blk, bsz, s_len, n_ctx, sink, window):
    ctx_blk0 = bsz * s_len // n_ctx
    if window:
        tq = 512
        nq = s_len // tq
        q_spec = pl.BlockSpec((tq, LANES), lambda b, h, i: (b * nq + i, q_cb(h)))
        kv_specs = [pl.BlockSpec((s_len, LANES), lambda b, h, i: (b, k_cb(h))),
                    pl.BlockSpec((s_len, LANES), lambda b, h, i: (b, v_cb(h)))]
        args = [q_arr, k_arr, v_arr, k_arr, v_arr]
        out_rows = bsz * s_len
    else:
        tq = n_ctx
        nq = 1
        q_spec = pl.BlockSpec((tq, LANES), lambda b, h, i: (ctx_blk0 + b, q_cb(h)))
        kv_specs = []
        args = [q_arr, k_arr, v_arr]
        out_rows = bsz * n_ctx
    kv_specs += [pl.BlockSpec((n_ctx, LANES), lambda b, h, i: (ctx_blk0 + b, k_cb(h))),
                 pl.BlockSpec((n_ctx, LANES), lambda b, h, i: (ctx_blk0 + b, v_cb(h)))]
    in_specs = [q_spec] + kv_specs
    if sink is not None:
        in_specs.append(pl.BlockSpec(memory_space=pltpu.SMEM))
        args.append(sink.astype(F32))
    return pl.pallas_call(
        functools.partial(_pair_attn_kernel, window=window, tq=tq, s_len=s_len,
                          has_sink=sink is not None, heads_per_group=n_blk),
        grid=(bsz, n_blk, nq),
        in_specs=in_specs,
        out_specs=pl.BlockSpec((tq, LANES), lambda b, h, i: (b * nq + i, h)),
        out_shape=jax.ShapeDtypeStruct((out_rows, n_blk * LANES), BF16),
        compiler_params=_cparams("parallel", "parallel", "arbitrary"),
        name="pair_attn_window" if window else "pair_attn_ctx",
    )(*args)


def _mla_kernel(*refs, latent, tk, n_chunks, sub):
    if latent:
        q_ref, kl_ref, vl_ref, kc_ref, vc_ref, o_ref = refs
    else:
        q_ref, kc_ref, vc_ref, o_ref = refs
    streams = [(j, r) for j in range(2) for r in range(q_ref.shape[0] // sub)]

    def update(j, r, k, v, state):
        m, l, acc = state
        s = _dot_nt(q_ref[r * sub:(r + 1) * sub, LANES * j:LANES * (j + 1)], k)
        m_new = jnp.maximum(m, jnp.max(s, axis=1, keepdims=True))
        p = jnp.exp2(s - m_new)
        alpha = jnp.exp2(m - m_new)
        l_new = alpha * l + jnp.sum(p, axis=1, keepdims=True)
        acc_new = alpha * acc + _dot(p.astype(BF16), v)
        return m_new, l_new, acc_new

    init = (jnp.full((sub, 1), NEG_INF, F32), jnp.zeros((sub, 1), F32), jnp.zeros((sub, LANES), F32))
    vc = vc_ref[...]
    state = tuple(update(j, r, kc_ref[:, LANES * j:LANES * (j + 1)], vc, init) for j, r in streams)
    if latent:
        for c in range(n_chunks):
            v = vl_ref[c * tk:(c + 1) * tk, :]
            state = tuple(update(j, r, kl_ref[c * tk:(c + 1) * tk, LANES * j:LANES * (j + 1)], v, st)
                          for (j, r), st in zip(streams, state))
    for r in range(q_ref.shape[0] // sub):
        o0, o1 = [state[streams.index((j, r))] for j in range(2)]
        o0 = o0[2] / o0[1]
        o1 = o1[2] / o1[1]
        o_ref[r * sub:(r + 1) * sub, :] = jnp.where(_half_mask(o0.shape, 0), o0, o1).astype(o_ref.dtype)


def _mla_attn(arr, q_col, k_col, v_col, bsz, s_len, n_ctx, latent):
    n_pair = B_HEADS // 2
    ctx_blk0 = bsz * s_len // n_ctx
    qc, kc, vc = q_col // (2 * LANES), k_col // (2 * LANES), v_col // LANES
    kv_specs = [pl.BlockSpec((n_ctx, 2 * LANES), lambda b, h, i: (ctx_blk0 + b, kc + h)),
                pl.BlockSpec((n_ctx, LANES), lambda b, h, i: (ctx_blk0 + b, vc + h))]
    if latent:
        tq, tk = 512, min(2048, s_len)
        nq = s_len // tq
        q_spec = pl.BlockSpec((tq, 2 * LANES), lambda b, h, i: (b * nq + i, qc + h))
        kv_specs = [pl.BlockSpec((s_len, 2 * LANES), lambda b, h, i: (b, kc + h)),
                    pl.BlockSpec((s_len, LANES), lambda b, h, i: (b, vc + h))] + kv_specs
        args = [arr] * 5
        out_rows = bsz * s_len
    else:
        tq, tk = n_ctx, n_ctx
        nq = 1
        q_spec = pl.BlockSpec((tq, 2 * LANES), lambda b, h, i: (ctx_blk0 + b, qc + h))
        args = [arr] * 3
        out_rows = bsz * n_ctx
    return pl.pallas_call(
        functools.partial(_mla_kernel, latent=latent, tk=tk, n_chunks=s_len // tk, sub=min(tq, 512)),
        grid=(bsz, n_pair, nq),
        in_specs=[q_spec] + kv_specs,
        out_specs=pl.BlockSpec((tq, LANES), lambda b, h, i: (b * nq + i, h)),
        out_shape=jax.ShapeDtypeStruct((out_rows, n_pair * LANES), BF16),
        compiler_params=_cparams("parallel", "parallel", "arbitrary"),
        name="mla_latent" if latent else "mla_ctx",
    )(*args)


NA_Q_ROWS = 4
NA_K_ROWS = NA_Q_ROWS + NA_ROWS


def _na_kernel(pat_ref, start_ref, q_ref, kl_ref, vl_ref, kc_ref, vc_ref, bias_ref, o_ref):
    del pat_ref
    rb = pl.program_id(2)
    nk = NA_K_ROWS * GRID_W
    start = pl.multiple_of(start_ref[rb] * GRID_W, NA_Q_ROWS * GRID_W)
    kw = kl_ref[pl.ds(start, nk), :]
    vw = vl_ref[pl.ds(start, nk), :]
    kc = kc_ref[...]
    vc = vc_ref[...]
    q = q_ref[...]
    outs = []
    for j in range(2):
        qj = jnp.where(_half_mask(q.shape, j), q, jnp.zeros_like(q))
        sw = _dot_nt(qj, kw) + bias_ref[0, j]
        sc = _dot_nt(qj, kc)
        outs.append(_softmax_pv([sw, sc], [vw, vc]))
    o_ref[...] = jnp.where(_half_mask(outs[0].shape, 0), outs[0], outs[1]).astype(o_ref.dtype)


def _na_patterns(n_rows):
    kh = NA_ROWS
    n_rb = n_rows // NA_Q_ROWS
    starts, keys = [], []
    for rb in range(n_rb):
        r_a = rb * NA_Q_ROWS
        start = int(np.clip(r_a - NA_Q_ROWS, 0, n_rows - NA_K_ROWS))
        assert start % NA_Q_ROWS == 0
        rows = r_a + np.arange(NA_Q_ROWS)
        r0 = np.clip(rows - kh // 2, 0, n_rows - kh)
        assert start <= r0.min() and r0.max() + kh <= start + NA_K_ROWS
        starts.append(start)
        keys.append((r_a - start, tuple((r0 - start).tolist())))
    uniq = sorted(set(keys))
    pat = [uniq.index(k) for k in keys]
    return np.asarray(starts, np.int32), np.asarray(pat, np.int32), uniq


def _na_bias_table(rpb, uniq):
    n_dr, n_dc = 2 * NA_ROWS - 1, 2 * NA_COLS - 1
    i = np.arange(NA_Q_ROWS)[:, None]
    j = np.arange(NA_K_ROWS)[None, :]
    rsel = np.zeros((len(uniq), NA_Q_ROWS, NA_K_ROWS, n_dr), np.float32)
    rvalid = np.zeros((len(uniq), NA_Q_ROWS, NA_K_ROWS), bool)
    for p, (delta, r0_rel) in enumerate(uniq):
        r0_rel = np.asarray(r0_rel)[:, None]
        valid = (j >= r0_rel) & (j < r0_rel + NA_ROWS)
        dr = np.clip(j - (delta + i) + NA_ROWS - 1, 0, n_dr - 1)
        rsel[p] = np.eye(n_dr, dtype=np.float32)[dr] * valid[..., None]
        rvalid[p] = valid
    c = np.arange(GRID_W)[:, None]
    kc = np.arange(GRID_W)[None, :]
    c0 = np.clip(c - NA_COLS // 2, 0, GRID_W - NA_COLS)
    cvalid = (kc >= c0) & (kc < c0 + NA_COLS)
    dc = np.clip(kc - c + NA_COLS - 1, 0, n_dc - 1)
    csel = np.eye(n_dc, dtype=np.float32)[dc] * cvalid[..., None]
    hp = lax.Precision.HIGHEST
    tmp = jnp.einsum('hab,cqb->hacq', rpb.astype(F32), jnp.asarray(csel), precision=hp)
    bias = jnp.einsum('pija,hacq->phicjq', jnp.asarray(rsel), tmp, precision=hp)
    valid = rvalid[:, None, :, None, :, None] & cvalid[None, None, None, :, None, :]
    bias = jnp.where(jnp.asarray(valid), bias, NEG_INF)
    return bias.reshape(len(uniq), C_HEADS, NA_Q_ROWS * GRID_W, NA_K_ROWS * GRID_W)


def _na_attn(qkv, rpb, bsz, s_len, n_ctx):
    n_pair = C_HEADS // 2
    n_rows = s_len // GRID_W
    starts, pat, uniq = _na_patterns(n_rows)
    bias = _na_bias_table(rpb, uniq)
    n_rb = n_rows // NA_Q_ROWS
    tq = NA_Q_ROWS * GRID_W
    nk = NA_K_ROWS * GRID_W
    ctx_blk0 = bsz * s_len // n_ctx
    grid_spec = pltpu.PrefetchScalarGridSpec(
        num_scalar_prefetch=2,
        grid=(n_pair, bsz, n_rb),
        in_specs=[
            pl.BlockSpec((tq, LANES), lambda h, b, r, pat, st: (b * n_rb + r, h)),
            pl.BlockSpec((s_len, LANES), lambda h, b, r, pat, st: (b, n_pair + h)),
            pl.BlockSpec((s_len, LANES), lambda h, b, r, pat, st: (b, 2 * n_pair + h)),
            pl.BlockSpec((n_ctx, LANES), lambda h, b, r, pat, st: (ctx_blk0 + b, n_pair + h)),
            pl.BlockSpec((n_ctx, LANES), lambda h, b, r, pat, st: (ctx_blk0 + b, 2 * n_pair + h)),
            pl.BlockSpec((1, 2, tq, nk), lambda h, b, r, pat, st: (pat[r], h, 0, 0)),
        ],
        out_specs=pl.BlockSpec((tq, LANES), lambda h, b, r, pat, st: (b * n_rb + r, h)),
    )
    return pl.pallas_call(
        _na_kernel,
        grid_spec=grid_spec,
        out_shape=jax.ShapeDtypeStruct((bsz * s_len, n_pair * LANES), BF16),
        compiler_params=_cparams("parallel", "parallel", "arbitrary"),
        name="na_latent",
    )(jnp.asarray(pat), jnp.asarray(starts), qkv, qkv, qkv, qkv, qkv, bias)


MOE_BLOCK = 512
MOE_FC = 512


def _moe_kernel(be_ref, nu_ref, x_ref, w1_ref, b1_ref, w2_ref, b2_ref, o_ref, w1_bf, w2_bf):
    i = pl.program_id(0)

    @pl.when(i < nu_ref[0])
    def _():
        @pl.when((i == 0) | (be_ref[i] != be_ref[jnp.maximum(i - 1, 0)]))
        def _():
            w1_bf[...] = w1_ref[0].astype(BF16)
            w2_bf[...] = w2_ref[0].astype(BF16)

        x = x_ref[...]
        acc = None
        for c in range(D_EXPERT // MOE_FC):
            lo, hi = c * MOE_FC, (c + 1) * MOE_FC
            glu = _dot(x, w1_bf[:, lo:hi]) + b1_ref[0, :, lo:hi]
            lin = _dot(x, w1_bf[:, D_EXPERT + lo:D_EXPERT + hi]) + b1_ref[0, :, D_EXPERT + lo:D_EXPERT + hi]
            glu = jnp.minimum(glu, SWIGLU_LIMIT)
            lin = jnp.clip(lin, -SWIGLU_LIMIT, SWIGLU_LIMIT)
            act = glu * (1.0 / (1.0 + jnp.exp(-SWIGLU_ALPHA * glu))) * (lin + 1.0)
            y = _dot(act.astype(BF16), w2_bf[lo:hi, :])
            acc = y if acc is None else acc + y
        o_ref[...] = (acc + b2_ref[0]).astype(o_ref.dtype)


def _moe_experts(xs, blk_exp, n_used, w1, b1, w2, b2):
    n_slot, d = xs.shape
    n_blk = n_slot // MOE_BLOCK

    def blk(i, nu):
        return jnp.maximum(jnp.minimum(i, nu[0] - 1), 0)

    grid_spec = pltpu.PrefetchScalarGridSpec(
        num_scalar_prefetch=2,
        grid=(n_blk,),
        in_specs=[
            pl.BlockSpec((MOE_BLOCK, d), lambda i, be, nu: (blk(i, nu), 0)),
            pl.BlockSpec((1, d, 2 * D_EXPERT), lambda i, be, nu: (be[blk(i, nu)], 0, 0)),
            pl.BlockSpec((1, 1, 2 * D_EXPERT), lambda i, be, nu: (be[blk(i, nu)], 0, 0)),
            pl.BlockSpec((1, D_EXPERT, d), lambda i, be, nu: (be[blk(i, nu)], 0, 0)),
            pl.BlockSpec((1, 1, d), lambda i, be, nu: (be[blk(i, nu)], 0, 0)),
        ],
        out_specs=pl.BlockSpec((MOE_BLOCK, d), lambda i, be, nu: (blk(i, nu), 0)),
        scratch_shapes=[pltpu.VMEM((d, 2 * D_EXPERT), BF16), pltpu.VMEM((D_EXPERT, d), BF16)],
    )
    return pl.pallas_call(
        _moe_kernel,
        grid_spec=grid_spec,
        out_shape=jax.ShapeDtypeStruct((n_slot, d), BF16),
        compiler_params=pltpu.CompilerParams(dimension_semantics=("arbitrary",),
                                             vmem_limit_bytes=MOE_VMEM_LIMIT),
        name="moe_experts",
    )(blk_exp, n_used, xs, w1, b1.reshape(N_EXPERTS, 1, 2 * D_EXPERT), w2, b2.reshape(N_EXPERTS, 1, d))


def _combine_kernel(*refs, final):
    if final:
        x_ref, y_ref, g_ref, gate_ref, gf_ref, o_ref = refs
    else:
        x_ref, y_ref, g_ref, gate_ref, o_ref = refs
    g = g_ref[...]
    acc = y_ref[0].astype(F32) * g[:, 0:1]
    for k in range(1, TOP_K):
        acc = acc + y_ref[k].astype(F32) * g[:, k:k + 1]
    xn = x_ref[...] + gate_ref[0] * acc
    o_ref[...] = _rms(xn, gf_ref[...]) if final else xn


def _combine(x, yk, gates, mod, seg_of_tile, rows, final_g=None):
    d = x.shape[1]
    tm = _row_tile(rows)
    final = final_g is not None
    in_specs = [pl.BlockSpec((tm, d), lambda i: (i, 0)),
                pl.BlockSpec((TOP_K, tm, d), lambda i: (0, i, 0)),
                pl.BlockSpec((tm, TOP_K), lambda i: (i, 0)),
                pl.BlockSpec((1, 1, d), lambda i: (seg_of_tile(i, tm) * 6 + 5, 0, 0))]
    args = [x, yk, gates, mod]
    if final:
        in_specs.append(pl.BlockSpec((1, d), lambda i: (0, 0)))
        args.append(final_g.reshape(1, d))
    return pl.pallas_call(
        functools.partial(_combine_kernel, final=final),
        grid=(rows // tm,),
        in_specs=in_specs,
        out_specs=pl.BlockSpec((tm, d), lambda i: (i, 0)),
        out_shape=jax.ShapeDtypeStruct((rows, d), F32),
        compiler_params=_cparams("parallel"),
        name="moe_combine",
    )(*args)


def _moe(h, top_idx, rank, counts, w1, b1, w2, b2):
    t, d = h.shape
    n_asg = t * TOP_K
    padded = (counts + MOE_BLOCK - 1) // MOE_BLOCK * MOE_BLOCK
    pad_end = jnp.cumsum(padded)
    pad_start = pad_end - padded
    grp_start = jnp.cumsum(counts) - counts
    experts = jnp.arange(N_EXPERTS, dtype=jnp.int32)
    dest = rank + jnp.sum(jnp.where(top_idx[:, :, None] == experts, pad_start, 0), axis=-1)
    dest_flat = dest.reshape(n_asg)
    tok_flat = jnp.tile(jnp.arange(t, dtype=jnp.int32), TOP_K)
    _, tok_sorted = lax.sort((dest_flat, tok_flat), num_keys=1)
    n_blk = (n_asg + N_EXPERTS * (MOE_BLOCK - 1) + MOE_BLOCK - 1) // MOE_BLOCK
    blk_start = jnp.arange(n_blk, dtype=jnp.int32) * MOE_BLOCK
    blk_exp = jnp.minimum(jnp.sum((pad_end[None, :] <= blk_start[:, None]).astype(jnp.int32), axis=1),
                          N_EXPERTS - 1)
    n_used = (pad_end[-1:] // MOE_BLOCK).astype(jnp.int32)
    shift = jnp.repeat((grp_start - pad_start)[blk_exp], MOE_BLOCK)
    src = jnp.clip(jnp.arange(n_blk * MOE_BLOCK, dtype=jnp.int32) + shift, 0, n_asg - 1)
    ys = _moe_experts(h[tok_sorted[src]], blk_exp, n_used, w1, b1, w2, b2)
    return ys[dest_flat].reshape(TOP_K, t, d)


def _pad_cols(w, n):
    return jnp.pad(w, ((0, 0), (0, n - w.shape[1])))


A_HEAD_ORDER = tuple(h for blk in range(A_HEADS // 2) for h in (blk, blk + A_HEADS // 2))


def _even_layer_attn(x, mod, seg_of_tile, norm1_g, w_in, sink, q_norm_g, w_uq, kv_norm_g, w_ukv, w_out,
                     tables, bsz, s_len, n_ctx):
    d = x.shape[1]
    proj = _even_proj(x, mod, seg_of_tile, norm1_g, w_in, q_norm_g, w_uq, kv_norm_g, w_ukv, tables, bsz, s_len)
    n_blk = A_HEADS // 2
    q_cb = lambda h: EV_QA // LANES + h
    k_cb = lambda h: EV_KA // LANES
    v_cb = lambda h: EV_VA // LANES
    oa_l = _pair_attn(proj, q_cb, proj, k_cb, proj, v_cb, n_blk, bsz, s_len, n_ctx, sink, window=True)
    oa_c = _pair_attn(proj, q_cb, proj, k_cb, proj, v_cb, n_blk, bsz, s_len, n_ctx, sink, window=False)
    ob_l = _mla_attn(proj, EV_QB, EV_KB, EV_VB, bsz, s_len, n_ctx, latent=True)
    ob_c = _mla_attn(proj, EV_QB, EV_KB, EV_VB, bsz, s_len, n_ctx, latent=False)
    n_a = A_HEADS * HEAD_DIM
    w_oa = w_out[:n_a].reshape(A_HEADS, HEAD_DIM, d)[A_HEAD_ORDER, :, :].reshape(n_a, d)
    return [(oa_l, oa_c), (ob_l, ob_c)], [w_oa.astype(BF16), w_out[n_a:].astype(BF16)]


def _odd_layer_attn(x, mod, seg_of_tile, norm1_g, w_in, rpb, w_out, bsz, s_len, n_ctx):
    d = x.shape[1]
    width = C_HEADS * HEAD_DIM
    w1 = jnp.concatenate([w_in[:, :width] * (HEAD_DIM ** -0.5), w_in[:, width:]], axis=1)
    qkv = _norm_matmul(x, 0, d, norm1_g, w1.astype(BF16), mod, 0, 1, seg_of_tile)
    n_pair = C_HEADS // 2
    o_l = _na_attn(qkv, rpb, bsz, s_len, n_ctx)
    o_c = _pair_attn(qkv, lambda h: h, qkv, lambda h: n_pair + h, qkv, lambda h: 2 * n_pair + h,
                     n_pair, bsz, s_len, n_ctx, None, window=False)
    return [(o_l, o_c)], [w_out.astype(BF16)]


def kernel(x, c, ctx, c_ctx, ada_w, ada_b, norm1_g, norm2_g, ev_w_in, ev_sink, ev_q_norm_g, ev_w_uq,
           ev_kv_norm_g, ev_w_ukv, ev_w_out, od_w_in, od_rpb, od_w_out, router_w, router_b,
           exp_w1, exp_b1, exp_w2, exp_b2, final_g):
    bsz, s_len, d = x.shape
    n_ctx = ctx.shape[1]
    depth = ada_w.shape[0]
    n_lat = bsz * s_len
    assert bsz < MOD_ROWS and s_len % 512 == 0 and n_lat % n_ctx == 0

    def seg_of_tile(i, tm):
        return jnp.minimum(i * tm // s_len, bsz)

    c_rows = jnp.concatenate([c, c_ctx[None, :], jnp.zeros((MOD_ROWS - bsz - 1, d), F32)], axis=0)
    mods = _ada_modulation(c_rows, ada_w, ada_b)

    t = n_lat + bsz * n_ctx
    tables = _rope_tables(s_len, _row_tile(t))
    xs = jnp.concatenate([x.reshape(n_lat, d), ctx.reshape(bsz * n_ctx, d)], axis=0)
    for layer in range(depth):
        i = layer // 2
        last = layer == depth - 1
        mod = mods[layer].reshape(MOD_ROWS * 6, 1, d)
        if layer % 2 == 0:
            a_list, w_list = _even_layer_attn(xs, mod, seg_of_tile, norm1_g[layer], ev_w_in[i], ev_sink[i],
                                              ev_q_norm_g[i], ev_w_uq[i], ev_kv_norm_g[i], ev_w_ukv[i],
                                              ev_w_out[i], tables, bsz, s_len, n_ctx)
        else:
            a_list, w_list = _odd_layer_attn(xs, mod, seg_of_tile, norm1_g[layer], od_w_in[i], od_rpb[i],
                                             od_w_out[i], bsz, s_len, n_ctx)
        xs, h2, top_idx, top_gate, rank, counts = _out_router(a_list, w_list, xs, mod, norm2_g[layer],
                                                              router_w[layer], router_b[layer], seg_of_tile)
        yk = _moe(h2, top_idx, rank, counts[:, 0].astype(jnp.int32), exp_w1[layer], exp_b1[layer],
                  exp_w2[layer], exp_b2[layer])
        xs = _combine(xs, yk, top_gate.T, mod, seg_of_tile, n_lat if last else t, final_g if last else None)
    return xs.reshape(bsz, s_len, d)
```

```python
import functools

import numpy as np
import jax
import jax.numpy as jnp
from jax import lax
from jax.experimental import pallas as pl
from jax.experimental.pallas import tpu as pltpu

GRID_W = 64
HEAD_DIM = 64
ROPE_BASE = 10000.0
NORM_EPS = 1e-6
NEG_INF = -1e30

A_HEADS = 8
A_KV_HEADS = 2
A_WINDOW = 128
B_HEADS = 8
B_NOPE = 64
B_ROPE = 32
B_V = 64
B_Q_RANK = 768
B_KV_RANK = 256
C_HEADS = 16
NA_ROWS = 8
NA_COLS = 16

N_EXPERTS = 32
TOP_K = 4
D_EXPERT = 1024
SWIGLU_LIMIT = 7.0
SWIGLU_ALPHA = 1.702

LANES = 128
MOD_ROWS = 8
VMEM_LIMIT = 48 * 1024 * 1024
MOE_VMEM_LIMIT = 56 * 1024 * 1024

BF16 = jnp.bfloat16
F32 = jnp.float32
NT_DIMS = (((1,), (1,)), ((), ()))
LOG2_E = 1.4426950408889634


def _cparams(*sem):
    return pltpu.CompilerParams(dimension_semantics=sem, vmem_limit_bytes=VMEM_LIMIT)


def _dot(a, b):
    return jnp.dot(a, b, preferred_element_type=F32)


def _dot_nt(a, b):
    return lax.dot_general(a, b, NT_DIMS, preferred_element_type=F32)


def _ada_kernel(c_ref, w_ref, b_ref, o_ref):
    c = c_ref[...]
    s = c / (1.0 + jnp.exp(-c))
    o_ref[0] = jnp.dot(s, w_ref[0], preferred_element_type=F32, precision=lax.Precision.HIGHEST) + b_ref[0]


def _ada_modulation(c_rows, ada_w, ada_b):
    depth, d, n = ada_w.shape
    tn = 1024
    return pl.pallas_call(
        _ada_kernel,
        grid=(depth, n // tn),
        in_specs=[
            pl.BlockSpec((MOD_ROWS, d), lambda l, j: (0, 0)),
            pl.BlockSpec((1, d, tn), lambda l, j: (l, 0, j)),
            pl.BlockSpec((1, 1, tn), lambda l, j: (l, 0, j)),
        ],
        out_specs=pl.BlockSpec((1, MOD_ROWS, tn), lambda l, j: (l, 0, j)),
        out_shape=jax.ShapeDtypeStruct((depth, MOD_ROWS, n), F32),
        compiler_params=_cparams("parallel", "parallel"),
        name="ada_modulation",
    )(c_rows, ada_w, ada_b.reshape(depth, 1, n))


def _norm_mm_kernel(*refs, modulate):
    if modulate:
        x_ref, g_ref, sh_ref, sc_ref, w_ref, o_ref = refs
    else:
        x_ref, g_ref, w_ref, o_ref = refs
    x = x_ref[...].astype(F32)
    ms = jnp.mean(x * x, axis=-1, keepdims=True)
    h = x * lax.rsqrt(ms + NORM_EPS) * g_ref[...]
    if modulate:
        h = h * (1.0 + sc_ref[0]) + sh_ref[0]
    o_ref[...] = _dot(h.astype(BF16), w_ref[...]).astype(o_ref.dtype)


def _row_tile(t):
    return 512 if t % 512 == 0 else 256


def _norm_matmul(x, col_block, kdim, g, w, mod=None, shift_idx=0, scale_idx=0, seg_of_tile=None):
    t = x.shape[0]
    n = w.shape[1]
    tm = _row_tile(t)
    in_specs = [pl.BlockSpec((tm, kdim), lambda i: (i, col_block)),
                pl.BlockSpec((1, kdim), lambda i: (0, 0))]
    args = [x, g.reshape(1, kdim).astype(F32)]
    if mod is not None:
        in_specs += [pl.BlockSpec((1, 1, kdim), lambda i: (seg_of_tile(i, tm) * 6 + shift_idx, 0, 0)),
                     pl.BlockSpec((1, 1, kdim), lambda i: (seg_of_tile(i, tm) * 6 + scale_idx, 0, 0))]
        args += [mod, mod]
    in_specs.append(pl.BlockSpec((kdim, n), lambda i: (0, 0)))
    args.append(w)
    return pl.pallas_call(
        functools.partial(_norm_mm_kernel, modulate=mod is not None),
        grid=(t // tm,),
        in_specs=in_specs,
        out_specs=pl.BlockSpec((tm, n), lambda i: (i, 0)),
        out_shape=jax.ShapeDtypeStruct((t, n), BF16),
        compiler_params=_cparams("parallel"),
        name="norm_matmul",
    )(*args)


EV_QA, EV_KA, EV_VA = 0, 512, 640
EV_QB, EV_KB, EV_VB, EV_OUT = 768, 1792, 2816, 3328
P_QA, P_QA_ROT, P_KA, P_KA_ROT, P_VA, P_CQ, P_CKV, P_KPE, P_KPE_ROT, P_END = (
    0, 512, 1024, 1152, 1280, 1408, 2176, 2432, 2560, 2688)
TAB_COS64, TAB_SIN64, TAB_COSQ, TAB_SINQ, TAB_COSK, TAB_SINK, TAB_END = 0, 128, 256, 384, 512, 640, 768


def _rms(x, g):
    return x * lax.rsqrt(jnp.mean(x * x, axis=-1, keepdims=True) + NORM_EPS) * g


def _even_proj_kernel(x_ref, g_ref, sh_ref, sc_ref, w1_ref, tab_ref, gq_ref, wq_ref, gkv_ref, wk_ref, o_ref):
    h = _rms(x_ref[...], g_ref[...]) * (1.0 + sc_ref[0]) + sh_ref[0]
    p = _dot(h.astype(BF16), w1_ref[...])
    cos64, sin64 = tab_ref[:, TAB_COS64:TAB_SIN64], tab_ref[:, TAB_SIN64:TAB_COSQ]
    for b in range(A_HEADS // 2 + 1):
        lo = P_QA + LANES * b if b < A_HEADS // 2 else P_KA
        rot = P_QA_ROT + LANES * b if b < A_HEADS // 2 else P_KA_ROT
        o_ref[:, EV_QA + LANES * b:EV_QA + LANES * (b + 1)] = (
            p[:, lo:lo + LANES] * cos64 + p[:, rot:rot + LANES] * sin64).astype(BF16)
    o_ref[:, EV_VA:EV_QB] = p[:, P_VA:P_CQ].astype(BF16)

    cq = _rms(p[:, P_CQ:P_CKV], gq_ref[...]).astype(BF16)
    q2 = _dot(cq, wq_ref[...])
    cosq, sinq = tab_ref[:, TAB_COSQ:TAB_SINQ], tab_ref[:, TAB_SINQ:TAB_COSK]
    n_q = B_HEADS * LANES
    for hd in range(B_HEADS):
        lo = LANES * hd
        o_ref[:, EV_QB + lo:EV_QB + lo + LANES] = (
            q2[:, lo:lo + LANES] * cosq + q2[:, n_q + lo:n_q + lo + LANES] * sinq).astype(BF16)

    ckv = _rms(p[:, P_CKV:P_KPE], gkv_ref[...]).astype(BF16)
    kpe = (p[:, P_KPE:P_KPE_ROT] * tab_ref[:, TAB_COSK:TAB_SINK]
           + p[:, P_KPE_ROT:P_END] * tab_ref[:, TAB_SINK:TAB_END]).astype(BF16)
    o_ref[:, EV_KB:EV_OUT] = _dot(jnp.concatenate([ckv, kpe], axis=1), wk_ref[...]).astype(BF16)


def _rot_cols(w, half):
    g = w.reshape(w.shape[:-1] + (w.shape[-1] // (2 * half), 2, half))
    return jnp.stack([-g[..., 1, :], g[..., 0, :]], axis=-2).reshape(w.shape)


def _rope_tables(s_len, tm):
    t = jnp.arange(s_len, dtype=jnp.int32)
    pos = ((t // GRID_W).astype(F32)[:, None], (t % GRID_W).astype(F32)[:, None])

    def pattern(n):
        half = n // 4
        inv_freq = ROPE_BASE ** (-(jnp.arange(half, dtype=F32) / half))
        ang = jnp.concatenate([pos[0] * inv_freq, pos[0] * inv_freq, pos[1] * inv_freq, pos[1] * inv_freq], axis=1)
        return jnp.cos(ang), jnp.sin(ang)

    c64, s64 = pattern(HEAD_DIM)
    c32, s32 = pattern(B_ROPE)
    one = lambda n: jnp.ones((s_len, n), F32)
    zero = lambda n: jnp.zeros((s_len, n), F32)
    rest = LANES - B_NOPE - B_ROPE
    tab = jnp.concatenate([
        c64, c64, s64, s64,
        one(B_NOPE), c32, one(rest), zero(B_NOPE), s32, zero(rest),
        c32, one(LANES - B_ROPE), s32, zero(LANES - B_ROPE)], axis=1)
    ident = jnp.concatenate([jnp.ones((tm, LANES), F32), jnp.zeros((tm, LANES), F32)] * 3, axis=1)
    return jnp.concatenate([tab, ident], axis=0)


def _even_proj(x, mod, seg_of_tile, norm1_g, w_in, q_norm_g, w_uq, kv_norm_g, w_ukv, tables, bsz, s_len):
    t, d = x.shape
    tm = _row_tile(t)
    sizes = (A_HEADS * HEAD_DIM, A_KV_HEADS * HEAD_DIM, A_KV_HEADS * HEAD_DIM, B_Q_RANK, B_KV_RANK, B_ROPE)
    offs = np.cumsum((0,) + sizes)
    part = [w_in[:, offs[k]:offs[k + 1]] for k in range(6)]
    w_qa = part[0].reshape(d, A_HEADS, HEAD_DIM)[:, A_HEAD_ORDER, :].reshape(d, -1) * (HEAD_DIM ** -0.5 * LOG2_E)
    pad_blk = lambda w: _pad_cols(w, LANES)
    w1 = jnp.concatenate([w_qa, _rot_cols(w_qa, HEAD_DIM // 4), part[1], _rot_cols(part[1], HEAD_DIM // 4), part[2],
                          part[3], part[4], pad_blk(part[5]), pad_blk(_rot_cols(part[5], B_ROPE // 4))],
                         axis=1).astype(BF16)
    assert w1.shape[1] == P_END
    qk_dim = B_NOPE + B_ROPE
    rest = LANES - qk_dim
    wq = w_uq.reshape(B_Q_RANK, B_HEADS, qk_dim) * (qk_dim ** -0.5 * LOG2_E)
    wq_plain = jnp.pad(wq, ((0, 0), (0, 0), (0, rest)))
    wq_rot = jnp.pad(_rot_cols(wq[..., B_NOPE:], B_ROPE // 4), ((0, 0), (0, 0), (B_NOPE, rest)))
    wq2 = jnp.concatenate([wq_plain.reshape(B_Q_RANK, -1), wq_rot.reshape(B_Q_RANK, -1)], axis=1).astype(BF16)
    wkv = w_ukv.reshape(B_KV_RANK, B_HEADS, B_NOPE + B_V)
    wk = jnp.pad(wkv[..., :B_NOPE], ((0, 0), (0, 0), (0, LANES - B_NOPE))).reshape(B_KV_RANK, -1)
    place = jnp.pad(jnp.eye(B_ROPE, dtype=F32), ((0, LANES - B_ROPE), (B_NOPE, rest)))
    wk2 = jnp.concatenate([
        jnp.concatenate([wk, wkv[..., B_NOPE:].reshape(B_KV_RANK, -1)], axis=1),
        jnp.concatenate([jnp.tile(place, (1, B_HEADS)), jnp.zeros((LANES, B_HEADS * B_V), F32)], axis=1)],
        axis=0).astype(BF16)
    n_lat_tiles = bsz * s_len // tm
    per_batch = s_len // tm

    def tab_map(i):
        return (jnp.where(i < n_lat_tiles, i % per_batch, per_batch), 0)

    const = lambda i: (0, 0)
    return pl.pallas_call(
        _even_proj_kernel,
        grid=(t // tm,),
        in_specs=[
            pl.BlockSpec((tm, d), lambda i: (i, 0)),
            pl.BlockSpec((1, d), const),
            pl.BlockSpec((1, 1, d), lambda i: (seg_of_tile(i, tm) * 6 + 0, 0, 0)),
            pl.BlockSpec((1, 1, d), lambda i: (seg_of_tile(i, tm) * 6 + 1, 0, 0)),
            pl.BlockSpec(w1.shape, const),
            pl.BlockSpec((tm, TAB_END), tab_map),
            pl.BlockSpec((1, B_Q_RANK), const),
            pl.BlockSpec(wq2.shape, const),
            pl.BlockSpec((1, B_KV_RANK), const),
            pl.BlockSpec(wk2.shape, const),
        ],
        out_specs=pl.BlockSpec((tm, EV_OUT), lambda i: (i, 0)),
        out_shape=jax.ShapeDtypeStruct((t, EV_OUT), BF16),
        compiler_params=pltpu.CompilerParams(dimension_semantics=("parallel",), vmem_limit_bytes=MOE_VMEM_LIMIT),
        name="even_proj",
    )(x, norm1_g.reshape(1, d), mod, mod, w1, tables, q_norm_g.reshape(1, -1), wq2,
      kv_norm_g.reshape(1, -1), wk2)


def _out_router_kernel(*refs, n_a, n_lat_tiles):
    al_refs = refs[:n_a]
    ac_refs = refs[n_a:2 * n_a]
    w_refs = refs[2 * n_a:3 * n_a]
    x_ref, gate_ref, g2_ref, sh_ref, sc_ref, rwt_ref, rb_ref, tri_ref = refs[3 * n_a:3 * n_a + 8]
    xo_ref, h_ref, ti_ref, tg_ref, rk_ref, cnt_ref, cnt_scr = refs[3 * n_a + 8:]
    is_latent = pl.program_id(0) < n_lat_tiles
    acc = None
    for k in range(n_a):
        a = jnp.where(is_latent, al_refs[k][...], ac_refs[k][...])
        part = _dot(a, w_refs[k][...])
        acc = part if acc is None else acc + part
    xn = x_ref[...] + gate_ref[0] * acc
    xo_ref[...] = xn
    ms = jnp.mean(xn * xn, axis=-1, keepdims=True)
    h = xn * lax.rsqrt(ms + NORM_EPS) * g2_ref[...]
    h = h * (1.0 + sc_ref[0]) + sh_ref[0]
    h_ref[...] = h.astype(BF16)
    logits = lax.dot_general(rwt_ref[...], h, NT_DIMS, preferred_element_type=F32,
                             precision=lax.Precision.HIGHEST) + rb_ref[...]
    eidx = lax.broadcasted_iota(jnp.int32, logits.shape, 0)
    vals, idxs, hots = [], [], []
    cur = logits
    for _ in range(TOP_K):
        m = jnp.max(cur, axis=0, keepdims=True)
        idx = jnp.min(jnp.where(cur == m, eidx, N_EXPERTS), axis=0, keepdims=True)
        hot = eidx == idx
        vals.append(m)
        idxs.append(idx)
        hots.append(hot)
        cur = jnp.where(hot, -jnp.inf, cur)
    es = [jnp.exp(v - vals[0]) for v in vals]
    den = es[0] + es[1] + es[2] + es[3]
    ti_ref[...] = jnp.concatenate(idxs, axis=0)
    tg_ref[...] = jnp.concatenate([e / den for e in es], axis=0)

    @pl.when(pl.program_id(0) == 0)
    def _():
        cnt_scr[...] = jnp.zeros_like(cnt_scr)

    chosen = jnp.where(hots[0] | hots[1] | hots[2] | hots[3], 1.0, 0.0)
    before = cnt_scr[...] + _dot(chosen.astype(BF16), tri_ref[...])
    rk_ref[...] = jnp.concatenate(
        [jnp.sum(jnp.where(hot, before, 0.0), axis=0, keepdims=True) for hot in hots], axis=0).astype(jnp.int32)
    total = cnt_scr[...] + jnp.sum(chosen, axis=1, keepdims=True)
    cnt_scr[...] = total
    cnt_ref[...] = jnp.broadcast_to(total, cnt_ref.shape)


def _out_router(a_list, w_list, x, mod, g2, router_w, router_b, seg_of_tile):
    t, d = x.shape
    tm = _row_tile(t)
    n_a = len(a_list)
    n_lat_tiles = a_list[0][0].shape[0] // tm

    def mod_spec(idx):
        return pl.BlockSpec((1, 1, d), lambda i: (seg_of_tile(i, tm) * 6 + idx, 0, 0))

    in_specs = [pl.BlockSpec((tm, al.shape[1]), lambda i: (jnp.minimum(i, n_lat_tiles - 1), 0)) for al, _ in a_list]
    in_specs += [pl.BlockSpec((tm, ac.shape[1]), lambda i: (jnp.maximum(i - n_lat_tiles, 0), 0)) for _, ac in a_list]
    in_specs += [pl.BlockSpec(w.shape, lambda i: (0, 0)) for w in w_list]
    in_specs += [pl.BlockSpec((tm, d), lambda i: (i, 0)), mod_spec(2),
                 pl.BlockSpec((1, d), lambda i: (0, 0)), mod_spec(3), mod_spec(4),
                 pl.BlockSpec((N_EXPERTS, d), lambda i: (0, 0)),
                 pl.BlockSpec((N_EXPERTS, 1), lambda i: (0, 0)),
                 pl.BlockSpec((tm, tm), lambda i: (0, 0))]
    strictly_upper = jnp.asarray(np.triu(np.ones((tm, tm), np.float32), k=1), BF16)
    return pl.pallas_call(
        functools.partial(_out_router_kernel, n_a=n_a, n_lat_tiles=n_lat_tiles),
        grid=(t // tm,),
        in_specs=in_specs,
        out_specs=[pl.BlockSpec((tm, d), lambda i: (i, 0)),
                   pl.BlockSpec((tm, d), lambda i: (i, 0)),
                   pl.BlockSpec((TOP_K, tm), lambda i: (0, i)),
                   pl.BlockSpec((TOP_K, tm), lambda i: (0, i)),
                   pl.BlockSpec((TOP_K, tm), lambda i: (0, i)),
                   pl.BlockSpec((N_EXPERTS, LANES), lambda i: (0, 0))],
        out_shape=[jax.ShapeDtypeStruct((t, d), F32),
                   jax.ShapeDtypeStruct((t, d), BF16),
                   jax.ShapeDtypeStruct((TOP_K, t), jnp.int32),
                   jax.ShapeDtypeStruct((TOP_K, t), F32),
                   jax.ShapeDtypeStruct((TOP_K, t), jnp.int32),
                   jax.ShapeDtypeStruct((N_EXPERTS, LANES), F32)],
        scratch_shapes=[pltpu.VMEM((N_EXPERTS, 1), F32)],
        compiler_params=_cparams("arbitrary"),
        name="out_router",
    )(*[al for al, _ in a_list], *[ac for _, ac in a_list], *w_list, x, mod, g2.reshape(1, d), mod, mod,
      router_w.T, router_b.reshape(N_EXPERTS, 1), strictly_upper)


def _half_mask(shape, j):
    lane = lax.broadcasted_iota(jnp.int32, shape, 1)
    return (lane >= HEAD_DIM * j) & (lane < HEAD_DIM * (j + 1))


def _softmax_pv(scores, values, sink=None):
    m = jnp.max(scores[0], axis=1, keepdims=True)
    for s in scores[1:]:
        m = jnp.maximum(m, jnp.max(s, axis=1, keepdims=True))
    if sink is not None:
        m = jnp.maximum(m, sink)
    den = None
    out = None
    for s, v in zip(scores, values):
        p = jnp.exp2(s - m)
        ps = jnp.sum(p, axis=1, keepdims=True)
        den = ps if den is None else den + ps
        o = _dot(p.astype(BF16), v)
        out = o if out is None else out + o
    if sink is not None:
        den = den + jnp.exp2(sink - m)
    return out / den


def _pair_attn_kernel(*refs, window, tq, s_len, has_sink, heads_per_group):
    if window:
        q_ref, kl_ref, vl_ref, kc_ref, vc_ref = refs[:5]
        rest = refs[5:]
    else:
        q_ref, kc_ref, vc_ref = refs[:3]
        rest = refs[3:]
    if has_sink:
        sink_ref, o_ref = rest
    else:
        (o_ref,) = rest
    blk = pl.program_id(1)
    q = q_ref[...]
    kc = kc_ref[...]
    vc = vc_ref[...]
    if window:
        i = pl.program_id(2)
        wl = tq + 2 * A_WINDOW
        start = pl.multiple_of(jnp.clip(i * tq - A_WINDOW, 0, s_len - wl), LANES)
        kw = kl_ref[pl.ds(start, wl), :]
        vw = vl_ref[pl.ds(start, wl), :]
        qpos = i * tq + lax.broadcasted_iota(jnp.int32, (tq, wl), 0)
        kpos = start + lax.broadcasted_iota(jnp.int32, (tq, wl), 1)
        band = jnp.abs(qpos - kpos) <= A_WINDOW
    outs = []
    for j in range(2):
        qj = jnp.where(_half_mask(q.shape, j), q, jnp.zeros_like(q))
        scores, values = [], []
        if window:
            scores.append(jnp.where(band, _dot_nt(qj, kw), NEG_INF))
            values.append(vw)
        scores.append(_dot_nt(qj, kc))
        values.append(vc)
        sink = sink_ref[j * heads_per_group + blk] if has_sink else None
        outs.append(_softmax_pv(scores, values, sink))
    o_ref[...] = jnp.where(_half_mask(outs[0].shape, 0), outs[0], outs[1]).astype(o_ref.dtype)


def _pair_attn(q_arr, q_cb, k_arr, k_cb, v_arr, v_cb, n_blk, bsz, s_len, n_ctx, sink, window):
    ctx_blk0 = bsz * s_len // n_ctx
    if window:
        tq = 512
        nq = s_len // tq
        q_spec = pl.BlockSpec((tq, LANES), lambda b, h, i: (b * nq + i, q_cb(h)))
        kv_specs = [pl.BlockSpec((s_len, LANES), lambda b, h, i: (b, k_cb(h))),
                    pl.BlockSpec((s_len, LANES), lambda b, h, i: (b, v_cb(h)))]
        args = [q_arr, k_arr, v_arr, k_arr, v_arr]
        out_rows = bsz * s_len
    else:
        tq = n_ctx
        nq = 1
        q_spec = pl.BlockSpec((tq, LANES), lambda b, h, i: (ctx_blk0 + b, q_cb(h)))
        kv_specs = []
        args = [q_arr, k_arr, v_arr]
        out_rows = bsz * n_ctx
    kv_specs += [pl.BlockSpec((n_ctx, LANES), lambda b, h, i: (ctx_blk0 + b, k_cb(h))),
                 pl.BlockSpec((n_ctx, LANES), lambda b, h, i: (ctx_blk0 + b, v_cb(h)))]
    in_specs = [q_spec] + kv_specs
    if sink is not None:
        in_specs.append(pl.BlockSpec(memory_space=pltpu.SMEM))
        args.append(sink.astype(F32) * LOG2_E)
    return pl.pallas_call(
        functools.partial(_pair_attn_kernel, window=window, tq=tq, s_len=s_len,
                          has_sink=sink is not None, heads_per_group=n_blk),
        grid=(bsz, n_blk, nq),
        in_specs=in_specs,
        out_specs=pl.BlockSpec((tq, LANES), lambda b, h, i: (b * nq + i, h)),
        out_shape=jax.ShapeDtypeStruct((out_rows, n_blk * LANES), BF16),
        compiler_params=_cparams("parallel", "parallel", "arbitrary"),
        name="pair_attn_window" if window else "pair_attn_ctx",
    )(*args)


def _mla_kernel(*refs, latent, tk, n_chunks, sub):
    if latent:
        q_ref, kl_ref, vl_ref, kc_ref, vc_ref, o_ref = refs
    else:
        q_ref, kc_ref, vc_ref, o_ref = refs
    streams = [(j, r) for j in range(2) for r in range(q_ref.shape[0] // sub)]

    def update(j, r, k, v, state):
        m, l, acc = state
        s = _dot_nt(q_ref[r * sub:(r + 1) * sub, LANES * j:LANES * (j + 1)], k)
        m_new = jnp.maximum(m, jnp.max(s, axis=1, keepdims=True))
        p = jnp.exp2(s - m_new)
        alpha = jnp.exp2(m - m_new)
        l_new = alpha * l + jnp.sum(p, axis=1, keepdims=True)
        acc_new = alpha * acc + _dot(p.astype(BF16), v)
        return m_new, l_new, acc_new

    init = (jnp.full((sub, 1), NEG_INF, F32), jnp.zeros((sub, 1), F32), jnp.zeros((sub, LANES), F32))
    vc = vc_ref[...]
    state = tuple(update(j, r, kc_ref[:, LANES * j:LANES * (j + 1)], vc, init) for j, r in streams)
    if latent:
        for c in range(n_chunks):
            v = vl_ref[c * tk:(c + 1) * tk, :]
            state = tuple(update(j, r, kl_ref[c * tk:(c + 1) * tk, LANES * j:LANES * (j + 1)], v, st)
                          for (j, r), st in zip(streams, state))
    for r in range(q_ref.shape[0] // sub):
        o0, o1 = [state[streams.index((j, r))] for j in range(2)]
        o0 = o0[2] / o0[1]
        o1 = o1[2] / o1[1]
        o_ref[r * sub:(r + 1) * sub, :] = jnp.where(_half_mask(o0.shape, 0), o0, o1).astype(o_ref.dtype)


def _mla_attn(arr, q_col, k_col, v_col, bsz, s_len, n_ctx, latent):
    n_pair = B_HEADS // 2
    ctx_blk0 = bsz * s_len // n_ctx
    qc, kc, vc = q_col // (2 * LANES), k_col // (2 * LANES), v_col // LANES
    kv_specs = [pl.BlockSpec((n_ctx, 2 * LANES), lambda b, h, i: (ctx_blk0 + b, kc + h)),
                pl.BlockSpec((n_ctx, LANES), lambda b, h, i: (ctx_blk0 + b, vc + h))]
    if latent:
        tq, tk = 512, min(2048, s_len)
        nq = s_len // tq
        q_spec = pl.BlockSpec((tq, 2 * LANES), lambda b, h, i: (b * nq + i, qc + h))
        kv_specs = [pl.BlockSpec((s_len, 2 * LANES), lambda b, h, i: (b, kc + h)),
                    pl.BlockSpec((s_len, LANES), lambda b, h, i: (b, vc + h))] + kv_specs
        args = [arr] * 5
        out_rows = bsz * s_len
    else:
        tq, tk = n_ctx, n_ctx
        nq = 1
        q_spec = pl.BlockSpec((tq, 2 * LANES), lambda b, h, i: (ctx_blk0 + b, qc + h))
        args = [arr] * 3
        out_rows = bsz * n_ctx
    return pl.pallas_call(
        functools.partial(_mla_kernel, latent=latent, tk=tk, n_chunks=s_len // tk, sub=min(tq, 512)),
        grid=(bsz, n_pair, nq),
        in_specs=[q_spec] + kv_specs,
        out_specs=pl.BlockSpec((tq, LANES), lambda b, h, i: (b * nq + i, h)),
        out_shape=jax.ShapeDtypeStruct((out_rows, n_pair * LANES), BF16),
        compiler_params=_cparams("parallel", "parallel", "arbitrary"),
        name="mla_latent" if latent else "mla_ctx",
    )(*args)


NA_Q_ROWS = 4
NA_K_ROWS = NA_Q_ROWS + NA_ROWS


NA_SUB = 4


def _na_kernel(pat_ref, start_ref, q_ref, kl_ref, vl_ref, kc_ref, vc_ref, *rest):
    del pat_ref
    bias_refs, o_ref = rest[:-1], rest[-1]
    rb = pl.program_id(2)
    nk = NA_K_ROWS * GRID_W
    tq = NA_Q_ROWS * GRID_W
    kc = kc_ref[...]
    vc = vc_ref[...]
    for u, bias_ref in enumerate(bias_refs):
        start = pl.multiple_of(start_ref[rb * len(bias_refs) + u] * GRID_W, NA_Q_ROWS * GRID_W)
        kw = kl_ref[pl.ds(start, nk), :]
        vw = vl_ref[pl.ds(start, nk), :]
        q = q_ref[u * tq:(u + 1) * tq, :]
        outs = []
        for j in range(2):
            qj = jnp.where(_half_mask(q.shape, j), q, jnp.zeros_like(q))
            sw = _dot_nt(qj, kw) + bias_ref[0, j]
            sc = _dot_nt(qj, kc)
            outs.append(_softmax_pv([sw, sc], [vw, vc]))
        o_ref[u * tq:(u + 1) * tq, :] = jnp.where(_half_mask(outs[0].shape, 0), outs[0], outs[1]).astype(o_ref.dtype)


def _na_patterns(n_rows):
    kh = NA_ROWS
    n_rb = n_rows // NA_Q_ROWS
    starts, keys = [], []
    for rb in range(n_rb):
        r_a = rb * NA_Q_ROWS
        start = int(np.clip(r_a - NA_Q_ROWS, 0, n_rows - NA_K_ROWS))
        assert start % NA_Q_ROWS == 0
        rows = r_a + np.arange(NA_Q_ROWS)
        r0 = np.clip(rows - kh // 2, 0, n_rows - kh)
        assert start <= r0.min() and r0.max() + kh <= start + NA_K_ROWS
        starts.append(start)
        keys.append((r_a - start, tuple((r0 - start).tolist())))
    uniq = sorted(set(keys))
    pat = [uniq.index(k) for k in keys]
    return np.asarray(starts, np.int32), np.asarray(pat, np.int32), uniq


def _na_bias_table(rpb, uniq):
    n_dr, n_dc = 2 * NA_ROWS - 1, 2 * NA_COLS - 1
    i = np.arange(NA_Q_ROWS)[:, None]
    j = np.arange(NA_K_ROWS)[None, :]
    rsel = np.zeros((len(uniq), NA_Q_ROWS, NA_K_ROWS, n_dr), np.float32)
    rvalid = np.zeros((len(uniq), NA_Q_ROWS, NA_K_ROWS), bool)
    for p, (delta, r0_rel) in enumerate(uniq):
        r0_rel = np.asarray(r0_rel)[:, None]
        valid = (j >= r0_rel) & (j < r0_rel + NA_ROWS)
        dr = np.clip(j - (delta + i) + NA_ROWS - 1, 0, n_dr - 1)
        rsel[p] = np.eye(n_dr, dtype=np.float32)[dr] * valid[..., None]
        rvalid[p] = valid
    c = np.arange(GRID_W)[:, None]
    kc = np.arange(GRID_W)[None, :]
    c0 = np.clip(c - NA_COLS // 2, 0, GRID_W - NA_COLS)
    cvalid = (kc >= c0) & (kc < c0 + NA_COLS)
    dc = np.clip(kc - c + NA_COLS - 1, 0, n_dc - 1)
    csel = np.eye(n_dc, dtype=np.float32)[dc] * cvalid[..., None]
    hp = lax.Precision.HIGHEST
    tmp = jnp.einsum('hab,cqb->hacq', rpb.astype(F32), jnp.asarray(csel), precision=hp)
    bias = jnp.einsum('pija,hacq->phicjq', jnp.asarray(rsel), tmp, precision=hp)
    valid = rvalid[:, None, :, None, :, None] & cvalid[None, None, None, :, None, :]
    bias = jnp.where(jnp.asarray(valid), bias * LOG2_E, NEG_INF)
    return bias.reshape(len(uniq), C_HEADS, NA_Q_ROWS * GRID_W, NA_K_ROWS * GRID_W)


def _na_attn(qkv, rpb, bsz, s_len, n_ctx):
    n_pair = C_HEADS // 2
    n_rows = s_len // GRID_W
    starts, pat, uniq = _na_patterns(n_rows)
    bias = _na_bias_table(rpb, uniq)
    n_sub = min(NA_SUB, n_rows // NA_Q_ROWS)
    n_rb = n_rows // NA_Q_ROWS // n_sub
    tq = NA_Q_ROWS * GRID_W * n_sub
    nk = NA_K_ROWS * GRID_W
    ctx_blk0 = bsz * s_len // n_ctx
    bias_specs = [pl.BlockSpec((1, 2, tq // n_sub, nk), lambda h, b, r, pat, st, u=u: (pat[r * n_sub + u], h, 0, 0))
                  for u in range(n_sub)]
    grid_spec = pltpu.PrefetchScalarGridSpec(
        num_scalar_prefetch=2,
        grid=(n_pair, bsz, n_rb),
        in_specs=[
            pl.BlockSpec((tq, LANES), lambda h, b, r, pat, st: (b * n_rb + r, h)),
            pl.BlockSpec((s_len, LANES), lambda h, b, r, pat, st: (b, n_pair + h)),
            pl.BlockSpec((s_len, LANES), lambda h, b, r, pat, st: (b, 2 * n_pair + h)),
            pl.BlockSpec((n_ctx, LANES), lambda h, b, r, pat, st: (ctx_blk0 + b, n_pair + h)),
            pl.BlockSpec((n_ctx, LANES), lambda h, b, r, pat, st: (ctx_blk0 + b, 2 * n_pair + h)),
        ] + bias_specs,
        out_specs=pl.BlockSpec((tq, LANES), lambda h, b, r, pat, st: (b * n_rb + r, h)),
    )
    return pl.pallas_call(
        _na_kernel,
        grid_spec=grid_spec,
        out_shape=jax.ShapeDtypeStruct((bsz * s_len, n_pair * LANES), BF16),
        compiler_params=_cparams("parallel", "parallel", "arbitrary"),
        name="na_latent",
    )(jnp.asarray(pat), jnp.asarray(starts), qkv, qkv, qkv, qkv, qkv, *([bias] * n_sub))


MOE_BLOCK = 512
MOE_FC = 512
MOE_PARTS = 4


def _moe_kernel(*refs, chained):
    be_ref, nu_ref, x_ref, w1_ref, b1_ref, w2_ref, b2_ref = refs[:7]
    o_ref, w1_bf, w2_bf = refs[-3:]
    i = pl.program_id(0)

    @pl.when(i < nu_ref[0])
    def _():
        @pl.when((i == 0) | (be_ref[i] != be_ref[jnp.maximum(i - 1, 0)]))
        def _():
            w1_bf[...] = w1_ref[0, 0].astype(BF16)
            w2_bf[...] = w2_ref[0, 0].astype(BF16)

        x = x_ref[...]
        acc = None
        for c in range(D_EXPERT // MOE_FC):
            lo, hi = c * MOE_FC, (c + 1) * MOE_FC
            glu = _dot(x, w1_bf[:, lo:hi]) + b1_ref[0, 0, :, lo:hi]
            lin = _dot(x, w1_bf[:, D_EXPERT + lo:D_EXPERT + hi]) + b1_ref[0, 0, :, D_EXPERT + lo:D_EXPERT + hi]
            glu = jnp.minimum(glu, SWIGLU_LIMIT)
            lin = jnp.clip(lin, -SWIGLU_LIMIT, SWIGLU_LIMIT)
            act = glu * (1.0 / (1.0 + jnp.exp(-SWIGLU_ALPHA * glu))) * (lin + 1.0)
            y = _dot(act.astype(BF16), w2_bf[lo:hi, :])
            acc = y if acc is None else acc + y
        o_ref[...] = (acc + b2_ref[0, 0]).astype(o_ref.dtype)

    @pl.when((i == 0) & (nu_ref[0] <= 0))
    def _():
        o_ref[...] = jnp.zeros_like(o_ref)


def _moe_experts(xs, blk_exp, n_used, layer, w1, b1, w2, b2, n_slot, blk0, prev=None):
    d = xs.shape[1]
    n_blk = xs.shape[0] // MOE_BLOCK

    def blk(i, nu):
        return jnp.maximum(jnp.minimum(i, nu[0] - 1), 0)

    in_specs = [
        pl.BlockSpec((MOE_BLOCK, d), lambda i, be, nu: (blk(i, nu), 0)),
        pl.BlockSpec((1, 1, d, 2 * D_EXPERT), lambda i, be, nu: (layer, be[blk(i, nu)], 0, 0)),
        pl.BlockSpec((1, 1, 1, 2 * D_EXPERT), lambda i, be, nu: (layer, be[blk(i, nu)], 0, 0)),
        pl.BlockSpec((1, 1, D_EXPERT, d), lambda i, be, nu: (layer, be[blk(i, nu)], 0, 0)),
        pl.BlockSpec((1, 1, 1, d), lambda i, be, nu: (layer, be[blk(i, nu)], 0, 0)),
    ]
    depth = w1.shape[0]
    args = [blk_exp, n_used, xs, w1, b1.reshape(depth, N_EXPERTS, 1, 2 * D_EXPERT), w2,
            b2.reshape(depth, N_EXPERTS, 1, d)]
    aliases = {}
    if prev is not None:
        in_specs.append(pl.BlockSpec(memory_space=pl.ANY))
        aliases = {len(args): 0}
        args.append(prev)
    grid_spec = pltpu.PrefetchScalarGridSpec(
        num_scalar_prefetch=2,
        grid=(n_blk,),
        in_specs=in_specs,
        out_specs=pl.BlockSpec((MOE_BLOCK, d), lambda i, be, nu: (blk0 + blk(i, nu), 0)),
        scratch_shapes=[pltpu.VMEM((d, 2 * D_EXPERT), BF16), pltpu.VMEM((D_EXPERT, d), BF16)],
    )
    return pl.pallas_call(
        functools.partial(_moe_kernel, chained=prev is not None),
        grid_spec=grid_spec,
        out_shape=jax.ShapeDtypeStruct((n_slot, d), BF16),
        input_output_aliases=aliases,
        compiler_params=pltpu.CompilerParams(dimension_semantics=("arbitrary",),
                                             vmem_limit_bytes=MOE_VMEM_LIMIT),
        name="moe_experts",
    )(*args)


def _combine_kernel(*refs, final):
    if final:
        x_ref, y_ref, g_ref, gate_ref, gf_ref, o_ref = refs
    else:
        x_ref, y_ref, g_ref, gate_ref, o_ref = refs
    g = g_ref[...]
    acc = y_ref[0].astype(F32) * g[:, 0:1]
    for k in range(1, TOP_K):
        acc = acc + y_ref[k].astype(F32) * g[:, k:k + 1]
    xn = x_ref[...] + gate_ref[0] * acc
    o_ref[...] = _rms(xn, gf_ref[...]) if final else xn


def _combine(x, yk, gates, mod, seg_of_tile, rows, final_g=None):
    d = x.shape[1]
    tm = _row_tile(rows)
    final = final_g is not None
    in_specs = [pl.BlockSpec((tm, d), lambda i: (i, 0)),
                pl.BlockSpec((TOP_K, tm, d), lambda i: (0, i, 0)),
                pl.BlockSpec((tm, TOP_K), lambda i: (i, 0)),
                pl.BlockSpec((1, 1, d), lambda i: (seg_of_tile(i, tm) * 6 + 5, 0, 0))]
    args = [x, yk, gates, mod]
    if final:
        in_specs.append(pl.BlockSpec((1, d), lambda i: (0, 0)))
        args.append(final_g.reshape(1, d))
    return pl.pallas_call(
        functools.partial(_combine_kernel, final=final),
        grid=(rows // tm,),
        in_specs=in_specs,
        out_specs=pl.BlockSpec((tm, d), lambda i: (i, 0)),
        out_shape=jax.ShapeDtypeStruct((rows, d), F32),
        compiler_params=_cparams("parallel"),
        name="moe_combine",
    )(*args)


def _moe(h, top_idx, rank, counts, layer, w1, b1, w2, b2):
    t, d = h.shape
    n_asg = t * TOP_K
    padded = (counts + MOE_BLOCK - 1) // MOE_BLOCK * MOE_BLOCK
    pad_end = jnp.cumsum(padded)
    pad_start = pad_end - padded
    grp_start = jnp.cumsum(counts) - counts
    experts = jnp.arange(N_EXPERTS, dtype=jnp.int32)
    dest = rank + jnp.sum(jnp.where(top_idx[:, :, None] == experts, pad_start, 0), axis=-1)
    dest_flat = dest.reshape(n_asg)
    tok_flat = jnp.tile(jnp.arange(t, dtype=jnp.int32), TOP_K)
    _, tok_sorted = lax.sort((dest_flat, tok_flat), num_keys=1)
    n_blk = (n_asg + N_EXPERTS * (MOE_BLOCK - 1) + MOE_BLOCK - 1) // MOE_BLOCK
    n_blk = -(-n_blk // MOE_PARTS) * MOE_PARTS
    blk_start = jnp.arange(n_blk, dtype=jnp.int32) * MOE_BLOCK
    blk_exp = jnp.minimum(jnp.sum((pad_end[None, :] <= blk_start[:, None]).astype(jnp.int32), axis=1),
                          N_EXPERTS - 1)
    n_used = (pad_end[-1:] // MOE_BLOCK).astype(jnp.int32)
    shift = jnp.repeat((grp_start - pad_start)[blk_exp], MOE_BLOCK)
    src = jnp.clip(jnp.arange(n_blk * MOE_BLOCK, dtype=jnp.int32) + shift, 0, n_asg - 1)
    buf_tok = tok_sorted[src]
    per = n_blk // MOE_PARTS
    ys = None
    for part in range(MOE_PARTS):
        lo = part * per
        xs = h[buf_tok[lo * MOE_BLOCK:(lo + per) * MOE_BLOCK]]
        ys = _moe_experts(xs, blk_exp[lo:lo + per], jnp.clip(n_used - lo, 0, per), layer, w1, b1, w2, b2,
                          n_blk * MOE_BLOCK, lo, ys)
    return ys[dest_flat].reshape(TOP_K, t, d)


def _pad_cols(w, n):
    return jnp.pad(w, ((0, 0), (0, n - w.shape[1])))


A_HEAD_ORDER = tuple(h for blk in range(A_HEADS // 2) for h in (blk, blk + A_HEADS // 2))


def _even_layer_attn(x, mod, seg_of_tile, norm1_g, w_in, sink, q_norm_g, w_uq, kv_norm_g, w_ukv, w_out,
                     tables, bsz, s_len, n_ctx):
    d = x.shape[1]
    proj = _even_proj(x, mod, seg_of_tile, norm1_g, w_in, q_norm_g, w_uq, kv_norm_g, w_ukv, tables, bsz, s_len)
    n_blk = A_HEADS // 2
    q_cb = lambda h: EV_QA // LANES + h
    k_cb = lambda h: EV_KA // LANES
    v_cb = lambda h: EV_VA // LANES
    oa_l = _pair_attn(proj, q_cb, proj, k_cb, proj, v_cb, n_blk, bsz, s_len, n_ctx, sink, window=True)
    oa_c = _pair_attn(proj, q_cb, proj, k_cb, proj, v_cb, n_blk, bsz, s_len, n_ctx, sink, window=False)
    ob_l = _mla_attn(proj, EV_QB, EV_KB, EV_VB, bsz, s_len, n_ctx, latent=True)
    ob_c = _mla_attn(proj, EV_QB, EV_KB, EV_VB, bsz, s_len, n_ctx, latent=False)
    n_a = A_HEADS * HEAD_DIM
    w_oa = w_out[:n_a].reshape(A_HEADS, HEAD_DIM, d)[A_HEAD_ORDER, :, :].reshape(n_a, d)
    return [(oa_l, oa_c), (ob_l, ob_c)], [w_oa.astype(BF16), w_out[n_a:].astype(BF16)]


def _odd_layer_attn(x, mod, seg_of_tile, norm1_g, w_in, rpb, w_out, bsz, s_len, n_ctx):
    d = x.shape[1]
    width = C_HEADS * HEAD_DIM
    w1 = jnp.concatenate([w_in[:, :width] * (HEAD_DIM ** -0.5 * LOG2_E), w_in[:, width:]], axis=1)
    qkv = _norm_matmul(x, 0, d, norm1_g, w1.astype(BF16), mod, 0, 1, seg_of_tile)
    n_pair = C_HEADS // 2
    o_l = _na_attn(qkv, rpb, bsz, s_len, n_ctx)
    o_c = _pair_attn(qkv, lambda h: h, qkv, lambda h: n_pair + h, qkv, lambda h: 2 * n_pair + h,
                     n_pair, bsz, s_len, n_ctx, None, window=False)
    return [(o_l, o_c)], [w_out.astype(BF16)]


def kernel(x, c, ctx, c_ctx, ada_w, ada_b, norm1_g, norm2_g, ev_w_in, ev_sink, ev_q_norm_g, ev_w_uq,
           ev_kv_norm_g, ev_w_ukv, ev_w_out, od_w_in, od_rpb, od_w_out, router_w, router_b,
           exp_w1, exp_b1, exp_w2, exp_b2, final_g):
    bsz, s_len, d = x.shape
    n_ctx = ctx.shape[1]
    depth = ada_w.shape[0]
    n_lat = bsz * s_len
    assert bsz < MOD_ROWS and s_len % 512 == 0 and n_lat % n_ctx == 0

    def seg_of_tile(i, tm):
        return jnp.minimum(i * tm // s_len, bsz)

    c_rows = jnp.concatenate([c, c_ctx[None, :], jnp.zeros((MOD_ROWS - bsz - 1, d), F32)], axis=0)
    mods = _ada_modulation(c_rows, ada_w, ada_b)

    t = n_lat + bsz * n_ctx
    tables = _rope_tables(s_len, _row_tile(t))
    xs = jnp.concatenate([x.reshape(n_lat, d), ctx.reshape(bsz * n_ctx, d)], axis=0)
    for layer in range(depth):
        i = layer // 2
        last = layer == depth - 1
        mod = mods[layer].reshape(MOD_ROWS * 6, 1, d)
        if layer % 2 == 0:
            a_list, w_list = _even_layer_attn(xs, mod, seg_of_tile, norm1_g[layer], ev_w_in[i], ev_sink[i],
                                              ev_q_norm_g[i], ev_w_uq[i], ev_kv_norm_g[i], ev_w_ukv[i],
                                              ev_w_out[i], tables, bsz, s_len, n_ctx)
        else:
            a_list, w_list = _odd_layer_attn(xs, mod, seg_of_tile, norm1_g[layer], od_w_in[i], od_rpb[i],
                                             od_w_out[i], bsz, s_len, n_ctx)
        xs, h2, top_idx, top_gate, rank, counts = _out_router(a_list, w_list, xs, mod, norm2_g[layer],
                                                              router_w[layer], router_b[layer], seg_of_tile)
        yk = _moe(h2, top_idx, rank, counts[:, 0].astype(jnp.int32), layer, exp_w1, exp_b1, exp_w2, exp_b2)
        xs = _combine(xs, yk, top_gate.T, mod, seg_of_tile, n_lat if last else t, final_g if last else None)
    return xs.reshape(bsz, s_len, d)
```

```python
import functools

import numpy as np
import jax
import jax.numpy as jnp
from jax import lax
from jax.experimental import pallas as pl
from jax.experimental.pallas import tpu as pltpu

GRID_W = 64
HEAD_DIM = 64
ROPE_BASE = 10000.0
NORM_EPS = 1e-6
NEG_INF = -1e30

A_HEADS = 8
A_KV_HEADS = 2
A_WINDOW = 128
B_HEADS = 8
B_NOPE = 64
B_ROPE = 32
B_V = 64
B_Q_RANK = 768
B_KV_RANK = 256
C_HEADS = 16
NA_ROWS = 8
NA_COLS = 16

N_EXPERTS = 32
TOP_K = 4
D_EXPERT = 1024
SWIGLU_LIMIT = 7.0
SWIGLU_ALPHA = 1.702

LANES = 128
MOD_ROWS = 8
VMEM_LIMIT = 48 * 1024 * 1024
MOE_VMEM_LIMIT = 56 * 1024 * 1024

BF16 = jnp.bfloat16
F32 = jnp.float32
NT_DIMS = (((1,), (1,)), ((), ()))
LOG2_E = 1.4426950408889634


def _cparams(*sem):
    return pltpu.CompilerParams(dimension_semantics=sem, vmem_limit_bytes=VMEM_LIMIT)


def _dot(a, b):
    return jnp.dot(a, b, preferred_element_type=F32)


def _dot_nt(a, b):
    return lax.dot_general(a, b, NT_DIMS, preferred_element_type=F32)


def _ada_kernel(c_ref, w_ref, b_ref, o_ref):
    c = c_ref[...]
    s = c / (1.0 + jnp.exp(-c))
    o_ref[0] = jnp.dot(s, w_ref[0], preferred_element_type=F32, precision=lax.Precision.HIGHEST) + b_ref[0]


def _ada_modulation(c_rows, ada_w, ada_b):
    depth, d, n = ada_w.shape
    tn = 1024
    return pl.pallas_call(
        _ada_kernel,
        grid=(depth, n // tn),
        in_specs=[
            pl.BlockSpec((MOD_ROWS, d), lambda l, j: (0, 0)),
            pl.BlockSpec((1, d, tn), lambda l, j: (l, 0, j)),
            pl.BlockSpec((1, 1, tn), lambda l, j: (l, 0, j)),
        ],
        out_specs=pl.BlockSpec((1, MOD_ROWS, tn), lambda l, j: (l, 0, j)),
        out_shape=jax.ShapeDtypeStruct((depth, MOD_ROWS, n), F32),
        compiler_params=_cparams("parallel", "parallel"),
        name="ada_modulation",
    )(c_rows, ada_w, ada_b.reshape(depth, 1, n))


def _norm_mm_kernel(*refs, modulate):
    if modulate:
        x_ref, g_ref, sh_ref, sc_ref, w_ref, o_ref = refs
    else:
        x_ref, g_ref, w_ref, o_ref = refs
    x = x_ref[...].astype(F32)
    ms = jnp.mean(x * x, axis=-1, keepdims=True)
    h = x * lax.rsqrt(ms + NORM_EPS) * g_ref[...]
    if modulate:
        h = h * (1.0 + sc_ref[0]) + sh_ref[0]
    o_ref[...] = _dot(h.astype(BF16), w_ref[...]).astype(o_ref.dtype)


def _row_tile(t):
    return 512 if t % 512 == 0 else 256


def _norm_matmul(x, col_block, kdim, g, w, mod=None, shift_idx=0, scale_idx=0, seg_of_tile=None):
    t = x.shape[0]
    n = w.shape[1]
    tm = _row_tile(t)
    in_specs = [pl.BlockSpec((tm, kdim), lambda i: (i, col_block)),
                pl.BlockSpec((1, kdim), lambda i: (0, 0))]
    args = [x, g.reshape(1, kdim).astype(F32)]
    if mod is not None:
        in_specs += [pl.BlockSpec((1, 1, kdim), lambda i: (seg_of_tile(i, tm) * 6 + shift_idx, 0, 0)),
                     pl.BlockSpec((1, 1, kdim), lambda i: (seg_of_tile(i, tm) * 6 + scale_idx, 0, 0))]
        args += [mod, mod]
    in_specs.append(pl.BlockSpec((kdim, n), lambda i: (0, 0)))
    args.append(w)
    return pl.pallas_call(
        functools.partial(_norm_mm_kernel, modulate=mod is not None),
        grid=(t // tm,),
        in_specs=in_specs,
        out_specs=pl.BlockSpec((tm, n), lambda i: (i, 0)),
        out_shape=jax.ShapeDtypeStruct((t, n), BF16),
        compiler_params=_cparams("parallel"),
        name="norm_matmul",
    )(*args)


EV_QA, EV_KA, EV_VA = 0, 512, 640
EV_QB, EV_KB, EV_VB, EV_OUT = 768, 1792, 2816, 3328
P_QA, P_QA_ROT, P_KA, P_KA_ROT, P_VA, P_CQ, P_CKV, P_KPE, P_KPE_ROT, P_END = (
    0, 512, 1024, 1152, 1280, 1408, 2176, 2432, 2560, 2688)
TAB_COS64, TAB_SIN64, TAB_COSQ, TAB_SINQ, TAB_COSK, TAB_SINK, TAB_END = 0, 128, 256, 384, 512, 640, 768


def _rms(x, g):
    return x * lax.rsqrt(jnp.mean(x * x, axis=-1, keepdims=True) + NORM_EPS) * g


def _even_proj_kernel(x_ref, g_ref, sh_ref, sc_ref, w1_ref, tab_ref, gq_ref, wq_ref, gkv_ref, wk_ref, o_ref):
    h = _rms(x_ref[...], g_ref[...]) * (1.0 + sc_ref[0]) + sh_ref[0]
    p = _dot(h.astype(BF16), w1_ref[...])
    cos64, sin64 = tab_ref[:, TAB_COS64:TAB_SIN64], tab_ref[:, TAB_SIN64:TAB_COSQ]
    for b in range(A_HEADS // 2 + 1):
        lo = P_QA + LANES * b if b < A_HEADS // 2 else P_KA
        rot = P_QA_ROT + LANES * b if b < A_HEADS // 2 else P_KA_ROT
        o_ref[:, EV_QA + LANES * b:EV_QA + LANES * (b + 1)] = (
            p[:, lo:lo + LANES] * cos64 + p[:, rot:rot + LANES] * sin64).astype(BF16)
    o_ref[:, EV_VA:EV_QB] = p[:, P_VA:P_CQ].astype(BF16)

    cq = _rms(p[:, P_CQ:P_CKV], gq_ref[...]).astype(BF16)
    q2 = _dot(cq, wq_ref[...])
    cosq, sinq = tab_ref[:, TAB_COSQ:TAB_SINQ], tab_ref[:, TAB_SINQ:TAB_COSK]
    n_q = B_HEADS * LANES
    for hd in range(B_HEADS):
        lo = LANES * hd
        o_ref[:, EV_QB + lo:EV_QB + lo + LANES] = (
            q2[:, lo:lo + LANES] * cosq + q2[:, n_q + lo:n_q + lo + LANES] * sinq).astype(BF16)

    ckv = _rms(p[:, P_CKV:P_KPE], gkv_ref[...]).astype(BF16)
    kpe = (p[:, P_KPE:P_KPE_ROT] * tab_ref[:, TAB_COSK:TAB_SINK]
           + p[:, P_KPE_ROT:P_END] * tab_ref[:, TAB_SINK:TAB_END]).astype(BF16)
    o_ref[:, EV_KB:EV_OUT] = _dot(jnp.concatenate([ckv, kpe], axis=1), wk_ref[...]).astype(BF16)


def _rot_cols(w, half):
    g = w.reshape(w.shape[:-1] + (w.shape[-1] // (2 * half), 2, half))
    return jnp.stack([-g[..., 1, :], g[..., 0, :]], axis=-2).reshape(w.shape)


def _rope_tables(s_len, tm):
    t = jnp.arange(s_len, dtype=jnp.int32)
    pos = ((t // GRID_W).astype(F32)[:, None], (t % GRID_W).astype(F32)[:, None])

    def pattern(n):
        half = n // 4
        inv_freq = ROPE_BASE ** (-(jnp.arange(half, dtype=F32) / half))
        ang = jnp.concatenate([pos[0] * inv_freq, pos[0] * inv_freq, pos[1] * inv_freq, pos[1] * inv_freq], axis=1)
        return jnp.cos(ang), jnp.sin(ang)

    c64, s64 = pattern(HEAD_DIM)
    c32, s32 = pattern(B_ROPE)
    one = lambda n: jnp.ones((s_len, n), F32)
    zero = lambda n: jnp.zeros((s_len, n), F32)
    rest = LANES - B_NOPE - B_ROPE
    tab = jnp.concatenate([
        c64, c64, s64, s64,
        one(B_NOPE), c32, one(rest), zero(B_NOPE), s32, zero(rest),
        c32, one(LANES - B_ROPE), s32, zero(LANES - B_ROPE)], axis=1)
    ident = jnp.concatenate([jnp.ones((tm, LANES), F32), jnp.zeros((tm, LANES), F32)] * 3, axis=1)
    return jnp.concatenate([tab, ident], axis=0)


def _even_proj(x, mod, seg_of_tile, norm1_g, w_in, q_norm_g, w_uq, kv_norm_g, w_ukv, tables, bsz, s_len):
    t, d = x.shape
    tm = _row_tile(t)
    sizes = (A_HEADS * HEAD_DIM, A_KV_HEADS * HEAD_DIM, A_KV_HEADS * HEAD_DIM, B_Q_RANK, B_KV_RANK, B_ROPE)
    offs = np.cumsum((0,) + sizes)
    part = [w_in[:, offs[k]:offs[k + 1]] for k in range(6)]
    w_qa = part[0].reshape(d, A_HEADS, HEAD_DIM)[:, A_HEAD_ORDER, :].reshape(d, -1) * (HEAD_DIM ** -0.5 * LOG2_E)
    pad_blk = lambda w: _pad_cols(w, LANES)
    w1 = jnp.concatenate([w_qa, _rot_cols(w_qa, HEAD_DIM // 4), part[1], _rot_cols(part[1], HEAD_DIM // 4), part[2],
                          part[3], part[4], pad_blk(part[5]), pad_blk(_rot_cols(part[5], B_ROPE // 4))],
                         axis=1).astype(BF16)
    assert w1.shape[1] == P_END
    qk_dim = B_NOPE + B_ROPE
    rest = LANES - qk_dim
    wq = w_uq.reshape(B_Q_RANK, B_HEADS, qk_dim) * (qk_dim ** -0.5 * LOG2_E)
    wq_plain = jnp.pad(wq, ((0, 0), (0, 0), (0, rest)))
    wq_rot = jnp.pad(_rot_cols(wq[..., B_NOPE:], B_ROPE // 4), ((0, 0), (0, 0), (B_NOPE, rest)))
    wq2 = jnp.concatenate([wq_plain.reshape(B_Q_RANK, -1), wq_rot.reshape(B_Q_RANK, -1)], axis=1).astype(BF16)
    wkv = w_ukv.reshape(B_KV_RANK, B_HEADS, B_NOPE + B_V)
    wk = jnp.pad(wkv[..., :B_NOPE], ((0, 0), (0, 0), (0, LANES - B_NOPE))).reshape(B_KV_RANK, -1)
    place = jnp.pad(jnp.eye(B_ROPE, dtype=F32), ((0, LANES - B_ROPE), (B_NOPE, rest)))
    wk2 = jnp.concatenate([
        jnp.concatenate([wk, wkv[..., B_NOPE:].reshape(B_KV_RANK, -1)], axis=1),
        jnp.concatenate([jnp.tile(place, (1, B_HEADS)), jnp.zeros((LANES, B_HEADS * B_V), F32)], axis=1)],
        axis=0).astype(BF16)
    n_lat_tiles = bsz * s_len // tm
    per_batch = s_len // tm

    def tab_map(i):
        return (jnp.where(i < n_lat_tiles, i % per_batch, per_batch), 0)

    const = lambda i: (0, 0)
    return pl.pallas_call(
        _even_proj_kernel,
        grid=(t // tm,),
        in_specs=[
            pl.BlockSpec((tm, d), lambda i: (i, 0)),
            pl.BlockSpec((1, d), const),
            pl.BlockSpec((1, 1, d), lambda i: (seg_of_tile(i, tm) * 6 + 0, 0, 0)),
            pl.BlockSpec((1, 1, d), lambda i: (seg_of_tile(i, tm) * 6 + 1, 0, 0)),
            pl.BlockSpec(w1.shape, const),
            pl.BlockSpec((tm, TAB_END), tab_map),
            pl.BlockSpec((1, B_Q_RANK), const),
            pl.BlockSpec(wq2.shape, const),
            pl.BlockSpec((1, B_KV_RANK), const),
            pl.BlockSpec(wk2.shape, const),
        ],
        out_specs=pl.BlockSpec((tm, EV_OUT), lambda i: (i, 0)),
        out_shape=jax.ShapeDtypeStruct((t, EV_OUT), BF16),
        compiler_params=pltpu.CompilerParams(dimension_semantics=("parallel",), vmem_limit_bytes=MOE_VMEM_LIMIT),
        name="even_proj",
    )(x, norm1_g.reshape(1, d), mod, mod, w1, tables, q_norm_g.reshape(1, -1), wq2,
      kv_norm_g.reshape(1, -1), wk2)


def _out_router_kernel(*refs, n_a, n_lat_tiles):
    al_refs = refs[:n_a]
    ac_refs = refs[n_a:2 * n_a]
    w_refs = refs[2 * n_a:3 * n_a]
    x_ref, gate_ref, g2_ref, sh_ref, sc_ref, rwt_ref, rb_ref, tri_ref = refs[3 * n_a:3 * n_a + 8]
    xo_ref, h_ref, ti_ref, tg_ref, rk_ref, cnt_ref, cnt_scr = refs[3 * n_a + 8:]
    is_latent = pl.program_id(0) < n_lat_tiles
    acc = None
    for k in range(n_a):
        a = jnp.where(is_latent, al_refs[k][...], ac_refs[k][...])
        part = _dot(a, w_refs[k][...])
        acc = part if acc is None else acc + part
    xn = x_ref[...] + gate_ref[0] * acc
    xo_ref[...] = xn
    ms = jnp.mean(xn * xn, axis=-1, keepdims=True)
    h = xn * lax.rsqrt(ms + NORM_EPS) * g2_ref[...]
    h = h * (1.0 + sc_ref[0]) + sh_ref[0]
    h_ref[...] = h.astype(BF16)
    h_hi = h.astype(BF16)
    h_lo = (h - h_hi.astype(F32)).astype(BF16)
    both = _dot_nt(rwt_ref[...], h_hi)
    logits = (both[:N_EXPERTS] + both[N_EXPERTS:]) + _dot_nt(rwt_ref[:N_EXPERTS, :], h_lo) + rb_ref[...]
    eidx = lax.broadcasted_iota(jnp.int32, logits.shape, 0)
    vals, idxs, hots = [], [], []
    cur = logits
    for _ in range(TOP_K):
        m = jnp.max(cur, axis=0, keepdims=True)
        idx = jnp.min(jnp.where(cur == m, eidx, N_EXPERTS), axis=0, keepdims=True)
        hot = eidx == idx
        vals.append(m)
        idxs.append(idx)
        hots.append(hot)
        cur = jnp.where(hot, -jnp.inf, cur)
    es = [jnp.exp(v - vals[0]) for v in vals]
    den = es[0] + es[1] + es[2] + es[3]
    ti_ref[...] = jnp.concatenate(idxs, axis=0)
    tg_ref[...] = jnp.concatenate([e / den for e in es], axis=0)

    @pl.when(pl.program_id(0) == 0)
    def _():
        cnt_scr[...] = jnp.zeros_like(cnt_scr)

    chosen = jnp.where(hots[0] | hots[1] | hots[2] | hots[3], 1.0, 0.0)
    before = cnt_scr[...] + _dot(chosen.astype(BF16), tri_ref[...])
    rk_ref[...] = jnp.concatenate(
        [jnp.sum(jnp.where(hot, before, 0.0), axis=0, keepdims=True) for hot in hots], axis=0).astype(jnp.int32)
    total = cnt_scr[...] + jnp.sum(chosen, axis=1, keepdims=True)
    cnt_scr[...] = total
    cnt_ref[...] = jnp.broadcast_to(total, cnt_ref.shape)


def _out_router(a_list, w_list, x, mod, g2, router_w, router_b, seg_of_tile):
    t, d = x.shape
    tm = _row_tile(t)
    n_a = len(a_list)
    n_lat_tiles = a_list[0][0].shape[0] // tm

    def mod_spec(idx):
        return pl.BlockSpec((1, 1, d), lambda i: (seg_of_tile(i, tm) * 6 + idx, 0, 0))

    in_specs = [pl.BlockSpec((tm, al.shape[1]), lambda i: (jnp.minimum(i, n_lat_tiles - 1), 0)) for al, _ in a_list]
    in_specs += [pl.BlockSpec((tm, ac.shape[1]), lambda i: (jnp.maximum(i - n_lat_tiles, 0), 0)) for _, ac in a_list]
    in_specs += [pl.BlockSpec(w.shape, lambda i: (0, 0)) for w in w_list]
    in_specs += [pl.BlockSpec((tm, d), lambda i: (i, 0)), mod_spec(2),
                 pl.BlockSpec((1, d), lambda i: (0, 0)), mod_spec(3), mod_spec(4),
                 pl.BlockSpec((2 * N_EXPERTS, d), lambda i: (0, 0)),
                 pl.BlockSpec((N_EXPERTS, 1), lambda i: (0, 0)),
                 pl.BlockSpec((tm, tm), lambda i: (0, 0))]
    rw_hi = router_w.T.astype(BF16)
    rw_lo = (router_w.T - rw_hi.astype(F32)).astype(BF16)
    strictly_upper = jnp.asarray(np.triu(np.ones((tm, tm), np.float32), k=1), BF16)
    return pl.pallas_call(
        functools.partial(_out_router_kernel, n_a=n_a, n_lat_tiles=n_lat_tiles),
        grid=(t // tm,),
        in_specs=in_specs,
        out_specs=[pl.BlockSpec((tm, d), lambda i: (i, 0)),
                   pl.BlockSpec((tm, d), lambda i: (i, 0)),
                   pl.BlockSpec((TOP_K, tm), lambda i: (0, i)),
                   pl.BlockSpec((TOP_K, tm), lambda i: (0, i)),
                   pl.BlockSpec((TOP_K, tm), lambda i: (0, i)),
                   pl.BlockSpec((N_EXPERTS, LANES), lambda i: (0, 0))],
        out_shape=[jax.ShapeDtypeStruct((t, d), F32),
                   jax.ShapeDtypeStruct((t, d), BF16),
                   jax.ShapeDtypeStruct((TOP_K, t), jnp.int32),
                   jax.ShapeDtypeStruct((TOP_K, t), F32),
                   jax.ShapeDtypeStruct((TOP_K, t), jnp.int32),
                   jax.ShapeDtypeStruct((N_EXPERTS, LANES), F32)],
        scratch_shapes=[pltpu.VMEM((N_EXPERTS, 1), F32)],
        compiler_params=_cparams("arbitrary"),
        name="out_router",
    )(*[al for al, _ in a_list], *[ac for _, ac in a_list], *w_list, x, mod, g2.reshape(1, d), mod, mod,
      jnp.concatenate([rw_hi, rw_lo], axis=0), router_b.reshape(N_EXPERTS, 1), strictly_upper)


def _half_mask(shape, j):
    lane = lax.broadcasted_iota(jnp.int32, shape, 1)
    return (lane >= HEAD_DIM * j) & (lane < HEAD_DIM * (j + 1))


def _softmax_pv(scores, values, sink=None):
    m = jnp.max(scores[0], axis=1, keepdims=True)
    for s in scores[1:]:
        m = jnp.maximum(m, jnp.max(s, axis=1, keepdims=True))
    if sink is not None:
        m = jnp.maximum(m, sink)
    den = None
    out = None
    for s, v in zip(scores, values):
        p = jnp.exp2(s - m)
        ps = jnp.sum(p, axis=1, keepdims=True)
        den = ps if den is None else den + ps
        o = _dot(p.astype(BF16), v)
        out = o if out is None else out + o
    if sink is not None:
        den = den + jnp.exp2(sink - m)
    return out / den


def _pair_attn_kernel(*refs, window, tq, s_len, has_sink, heads_per_group):
    if window:
        q_ref, kl_ref, vl_ref, kc_ref, vc_ref = refs[:5]
        rest = refs[5:]
    else:
        q_ref, kc_ref, vc_ref = refs[:3]
        rest = refs[3:]
    if has_sink:
        sink_ref, o_ref = rest
    else:
        (o_ref,) = rest
    blk = pl.program_id(1)
    q = q_ref[...]
    kc = kc_ref[...]
    vc = vc_ref[...]
    if window:
        i = pl.program_id(2)
        wl = tq + 2 * A_WINDOW
        start = pl.multiple_of(jnp.clip(i * tq - A_WINDOW, 0, s_len - wl), LANES)
        kw = kl_ref[pl.ds(start, wl), :]
        vw = vl_ref[pl.ds(start, wl), :]
        qpos = i * tq + lax.broadcasted_iota(jnp.int32, (tq, wl), 0)
        kpos = start + lax.broadcasted_iota(jnp.int32, (tq, wl), 1)
        band = jnp.abs(qpos - kpos) <= A_WINDOW
    outs = []
    for j in range(2):
        qj = jnp.where(_half_mask(q.shape, j), q, jnp.zeros_like(q))
        scores, values = [], []
        if window:
            scores.append(jnp.where(band, _dot_nt(qj, kw), NEG_INF))
            values.append(vw)
        scores.append(_dot_nt(qj, kc))
        values.append(vc)
        sink = sink_ref[j * heads_per_group + blk] if has_sink else None
        outs.append(_softmax_pv(scores, values, sink))
    o_ref[...] = jnp.where(_half_mask(outs[0].shape, 0), outs[0], outs[1]).astype(o_ref.dtype)


def _pair_attn(q_arr, q_cb, k_arr, k_cb, v_arr, v_cb, n_blk, bsz, s_len, n_ctx, sink, window):
    ctx_blk0 = bsz * s_len // n_ctx
    if window:
        tq = 512
        nq = s_len // tq
        q_spec = pl.BlockSpec((tq, LANES), lambda b, h, i: (b * nq + i, q_cb(h)))
        kv_specs = [pl.BlockSpec((s_len, LANES), lambda b, h, i: (b, k_cb(h))),
                    pl.BlockSpec((s_len, LANES), lambda b, h, i: (b, v_cb(h)))]
        args = [q_arr, k_arr, v_arr, k_arr, v_arr]
        out_rows = bsz * s_len
    else:
        tq = n_ctx
        nq = 1
        q_spec = pl.BlockSpec((tq, LANES), lambda b, h, i: (ctx_blk0 + b, q_cb(h)))
        kv_specs = []
        args = [q_arr, k_arr, v_arr]
        out_rows = bsz * n_ctx
    kv_specs += [pl.BlockSpec((n_ctx, LANES), lambda b, h, i: (ctx_blk0 + b, k_cb(h))),
                 pl.BlockSpec((n_ctx, LANES), lambda b, h, i: (ctx_blk0 + b, v_cb(h)))]
    in_specs = [q_spec] + kv_specs
    if sink is not None:
        in_specs.append(pl.BlockSpec(memory_space=pltpu.SMEM))
        args.append(sink.astype(F32) * LOG2_E)
    return pl.pallas_call(
        functools.partial(_pair_attn_kernel, window=window, tq=tq, s_len=s_len,
                          has_sink=sink is not None, heads_per_group=n_blk),
        grid=(bsz, n_blk, nq),
        in_specs=in_specs,
        out_specs=pl.BlockSpec((tq, LANES), lambda b, h, i: (b * nq + i, h)),
        out_shape=jax.ShapeDtypeStruct((out_rows, n_blk * LANES), BF16),
        compiler_params=_cparams("parallel", "parallel", "arbitrary"),
        name="pair_attn_window" if window else "pair_attn_ctx",
    )(*args)


def _mla_kernel(*refs, latent, tk, n_chunks, sub):
    if latent:
        q_ref, kl_ref, vl_ref, kc_ref, vc_ref, o_ref = refs
    else:
        q_ref, kc_ref, vc_ref, o_ref = refs
    streams = [(j, r) for j in range(2) for r in range(q_ref.shape[0] // sub)]

    def update(j, r, k, v, state):
        m, l, acc = state
        s = _dot_nt(q_ref[r * sub:(r + 1) * sub, LANES * j:LANES * (j + 1)], k)
        m_new = jnp.maximum(m, jnp.max(s, axis=1, keepdims=True))
        p = jnp.exp2(s - m_new)
        alpha = jnp.exp2(m - m_new)
        l_new = alpha * l + jnp.sum(p, axis=1, keepdims=True)
        acc_new = alpha * acc + _dot(p.astype(BF16), v)
        return m_new, l_new, acc_new

    init = (jnp.full((sub, 1), NEG_INF, F32), jnp.zeros((sub, 1), F32), jnp.zeros((sub, LANES), F32))
    vc = vc_ref[...]
    state = tuple(update(j, r, kc_ref[:, LANES * j:LANES * (j + 1)], vc, init) for j, r in streams)
    if latent:
        for c in range(n_chunks):
            v = vl_ref[c * tk:(c + 1) * tk, :]
            state = tuple(update(j, r, kl_ref[c * tk:(c + 1) * tk, LANES * j:LANES * (j + 1)], v, st)
                          for (j, r), st in zip(streams, state))
    for r in range(q_ref.shape[0] // sub):
        o0, o1 = [state[streams.index((j, r))] for j in range(2)]
        o0 = o0[2] / o0[1]
        o1 = o1[2] / o1[1]
        o_ref[r * sub:(r + 1) * sub, :] = jnp.where(_half_mask(o0.shape, 0), o0, o1).astype(o_ref.dtype)


def _mla_attn(arr, q_col, k_col, v_col, bsz, s_len, n_ctx, latent):
    n_pair = B_HEADS // 2
    ctx_blk0 = bsz * s_len // n_ctx
    qc, kc, vc = q_col // (2 * LANES), k_col // (2 * LANES), v_col // LANES
    kv_specs = [pl.BlockSpec((n_ctx, 2 * LANES), lambda b, h, i: (ctx_blk0 + b, kc + h)),
                pl.BlockSpec((n_ctx, LANES), lambda b, h, i: (ctx_blk0 + b, vc + h))]
    if latent:
        tq, tk = 512, min(2048, s_len)
        nq = s_len // tq
        q_spec = pl.BlockSpec((tq, 2 * LANES), lambda b, h, i: (b * nq + i, qc + h))
        kv_specs = [pl.BlockSpec((s_len, 2 * LANES), lambda b, h, i: (b, kc + h)),
                    pl.BlockSpec((s_len, LANES), lambda b, h, i: (b, vc + h))] + kv_specs
        args = [arr] * 5
        out_rows = bsz * s_len
    else:
        tq, tk = n_ctx, n_ctx
        nq = 1
        q_spec = pl.BlockSpec((tq, 2 * LANES), lambda b, h, i: (ctx_blk0 + b, qc + h))
        args = [arr] * 3
        out_rows = bsz * n_ctx
    return pl.pallas_call(
        functools.partial(_mla_kernel, latent=latent, tk=tk, n_chunks=s_len // tk, sub=min(tq, 512)),
        grid=(bsz, n_pair, nq),
        in_specs=[q_spec] + kv_specs,
        out_specs=pl.BlockSpec((tq, LANES), lambda b, h, i: (b * nq + i, h)),
        out_shape=jax.ShapeDtypeStruct((out_rows, n_pair * LANES), BF16),
        compiler_params=_cparams("parallel", "parallel", "arbitrary"),
        name="mla_latent" if latent else "mla_ctx",
    )(*args)


NA_Q_ROWS = 4
NA_K_ROWS = NA_Q_ROWS + NA_ROWS


NA_SUB = 4


def _na_kernel(pat_ref, start_ref, q_ref, kl_ref, vl_ref, kc_ref, vc_ref, *rest):
    del pat_ref
    bias_refs, o_ref = rest[:-1], rest[-1]
    rb = pl.program_id(2)
    nk = NA_K_ROWS * GRID_W
    tq = NA_Q_ROWS * GRID_W
    kc = kc_ref[...]
    vc = vc_ref[...]
    for u, bias_ref in enumerate(bias_refs):
        start = pl.multiple_of(start_ref[rb * len(bias_refs) + u] * GRID_W, NA_Q_ROWS * GRID_W)
        kw = kl_ref[pl.ds(start, nk), :]
        vw = vl_ref[pl.ds(start, nk), :]
        q = q_ref[u * tq:(u + 1) * tq, :]
        outs = []
        for j in range(2):
            qj = jnp.where(_half_mask(q.shape, j), q, jnp.zeros_like(q))
            sw = _dot_nt(qj, kw) + bias_ref[0, j]
            sc = _dot_nt(qj, kc)
            outs.append(_softmax_pv([sw, sc], [vw, vc]))
        o_ref[u * tq:(u + 1) * tq, :] = jnp.where(_half_mask(outs[0].shape, 0), outs[0], outs[1]).astype(o_ref.dtype)


def _na_patterns(n_rows):
    kh = NA_ROWS
    n_rb = n_rows // NA_Q_ROWS
    starts, keys = [], []
    for rb in range(n_rb):
        r_a = rb * NA_Q_ROWS
        start = int(np.clip(r_a - NA_Q_ROWS, 0, n_rows - NA_K_ROWS))
        assert start % NA_Q_ROWS == 0
        rows = r_a + np.arange(NA_Q_ROWS)
        r0 = np.clip(rows - kh // 2, 0, n_rows - kh)
        assert start <= r0.min() and r0.max() + kh <= start + NA_K_ROWS
        starts.append(start)
        keys.append((r_a - start, tuple((r0 - start).tolist())))
    uniq = sorted(set(keys))
    pat = [uniq.index(k) for k in keys]
    return np.asarray(starts, np.int32), np.asarray(pat, np.int32), uniq


def _na_bias_table(rpb, uniq):
    n_dr, n_dc = 2 * NA_ROWS - 1, 2 * NA_COLS - 1
    i = np.arange(NA_Q_ROWS)[:, None]
    j = np.arange(NA_K_ROWS)[None, :]
    rsel = np.zeros((len(uniq), NA_Q_ROWS, NA_K_ROWS, n_dr), np.float32)
    rvalid = np.zeros((len(uniq), NA_Q_ROWS, NA_K_ROWS), bool)
    for p, (delta, r0_rel) in enumerate(uniq):
        r0_rel = np.asarray(r0_rel)[:, None]
        valid = (j >= r0_rel) & (j < r0_rel + NA_ROWS)
        dr = np.clip(j - (delta + i) + NA_ROWS - 1, 0, n_dr - 1)
        rsel[p] = np.eye(n_dr, dtype=np.float32)[dr] * valid[..., None]
        rvalid[p] = valid
    c = np.arange(GRID_W)[:, None]
    kc = np.arange(GRID_W)[None, :]
    c0 = np.clip(c - NA_COLS // 2, 0, GRID_W - NA_COLS)
    cvalid = (kc >= c0) & (kc < c0 + NA_COLS)
    dc = np.clip(kc - c + NA_COLS - 1, 0, n_dc - 1)
    csel = np.eye(n_dc, dtype=np.float32)[dc] * cvalid[..., None]
    hp = lax.Precision.HIGHEST
    tmp = jnp.einsum('hab,cqb->hacq', rpb.astype(F32), jnp.asarray(csel), precision=hp)
    bias = jnp.einsum('pija,hacq->phicjq', jnp.asarray(rsel), tmp, precision=hp)
    valid = rvalid[:, None, :, None, :, None] & cvalid[None, None, None, :, None, :]
    bias = jnp.where(jnp.asarray(valid), bias * LOG2_E, NEG_INF)
    return bias.reshape(len(uniq), C_HEADS, NA_Q_ROWS * GRID_W, NA_K_ROWS * GRID_W)


def _na_attn(qkv, rpb, bsz, s_len, n_ctx):
    n_pair = C_HEADS // 2
    n_rows = s_len // GRID_W
    starts, pat, uniq = _na_patterns(n_rows)
    bias = _na_bias_table(rpb, uniq)
    n_sub = min(NA_SUB, n_rows // NA_Q_ROWS)
    n_rb = n_rows // NA_Q_ROWS // n_sub
    tq = NA_Q_ROWS * GRID_W * n_sub
    nk = NA_K_ROWS * GRID_W
    ctx_blk0 = bsz * s_len // n_ctx
    bias_specs = [pl.BlockSpec((1, 2, tq // n_sub, nk), lambda h, b, r, pat, st, u=u: (pat[r * n_sub + u], h, 0, 0))
                  for u in range(n_sub)]
    grid_spec = pltpu.PrefetchScalarGridSpec(
        num_scalar_prefetch=2,
        grid=(n_pair, bsz, n_rb),
        in_specs=[
            pl.BlockSpec((tq, LANES), lambda h, b, r, pat, st: (b * n_rb + r, h)),
            pl.BlockSpec((s_len, LANES), lambda h, b, r, pat, st: (b, n_pair + h)),
            pl.BlockSpec((s_len, LANES), lambda h, b, r, pat, st: (b, 2 * n_pair + h)),
            pl.BlockSpec((n_ctx, LANES), lambda h, b, r, pat, st: (ctx_blk0 + b, n_pair + h)),
            pl.BlockSpec((n_ctx, LANES), lambda h, b, r, pat, st: (ctx_blk0 + b, 2 * n_pair + h)),
        ] + bias_specs,
        out_specs=pl.BlockSpec((tq, LANES), lambda h, b, r, pat, st: (b * n_rb + r, h)),
    )
    return pl.pallas_call(
        _na_kernel,
        grid_spec=grid_spec,
        out_shape=jax.ShapeDtypeStruct((bsz * s_len, n_pair * LANES), BF16),
        compiler_params=_cparams("parallel", "parallel", "arbitrary"),
        name="na_latent",
    )(jnp.asarray(pat), jnp.asarray(starts), qkv, qkv, qkv, qkv, qkv, *([bias] * n_sub))


MOE_BLOCK = 512
MOE_FC = 512
MOE_PARTS = 4


def _moe_kernel(*refs, layer):
    be_ref, nu_ref, first_ref, slot_ref, next_ref, x_ref, b1_ref, b2_ref, w1_hbm, w2_hbm = refs[:10]
    o_ref, w1_f32, w2_f32, w1_bf, w2_bf, sem = refs[-6:]
    i = pl.program_id(0)

    def weight_copies(expert, slot):
        return (pltpu.make_async_copy(w1_hbm.at[layer, expert], w1_f32.at[slot], sem.at[0, slot]),
                pltpu.make_async_copy(w2_hbm.at[layer, expert], w2_f32.at[slot], sem.at[1, slot]))

    @pl.when(i < nu_ref[0])
    def _():
        @pl.when(first_ref[i] == 1)
        def _():
            slot = slot_ref[i]

            @pl.when(i == 0)
            def _():
                for cp in weight_copies(be_ref[i], slot):
                    cp.start()

            for cp in weight_copies(be_ref[i], slot):
                cp.wait()

            @pl.when(next_ref[i] >= 0)
            def _():
                for cp in weight_copies(next_ref[i], 1 - slot):
                    cp.start()

            w1_bf[...] = w1_f32[slot].astype(BF16)
            w2_bf[...] = w2_f32[slot].astype(BF16)

        x = x_ref[...]
        acc = None
        for c in range(D_EXPERT // MOE_FC):
            lo, hi = c * MOE_FC, (c + 1) * MOE_FC
            glu = _dot(x, w1_bf[:, lo:hi]) + b1_ref[0, 0, :, lo:hi]
            lin = _dot(x, w1_bf[:, D_EXPERT + lo:D_EXPERT + hi]) + b1_ref[0, 0, :, D_EXPERT + lo:D_EXPERT + hi]
            glu = jnp.minimum(glu, SWIGLU_LIMIT)
            lin = jnp.clip(lin, -SWIGLU_LIMIT, SWIGLU_LIMIT)
            act = glu * (1.0 / (1.0 + jnp.exp(-SWIGLU_ALPHA * glu))) * (lin + 1.0)
            y = _dot(act.astype(BF16), w2_bf[lo:hi, :])
            acc = y if acc is None else acc + y
        o_ref[...] = (acc + b2_ref[0, 0]).astype(o_ref.dtype)

    @pl.when((i == 0) & (nu_ref[0] <= 0))
    def _():
        o_ref[...] = jnp.zeros_like(o_ref)


def _moe_experts(xs, blk_exp, n_used, layer, w1, b1, w2, b2, n_slot, blk0, prev=None):
    d = xs.shape[1]
    n_blk = xs.shape[0] // MOE_BLOCK

    idx = jnp.arange(n_blk, dtype=jnp.int32)
    prev_exp = jnp.concatenate([jnp.full((1,), -1, jnp.int32), blk_exp[:-1]])
    first = (((idx == 0) | (blk_exp != prev_exp)) & (idx < n_used[0])).astype(jnp.int32)
    slot = (jnp.cumsum(first) - 1) % 2
    starts = jnp.where(first == 1, idx, n_blk)
    next_start = lax.cummin(jnp.concatenate([starts[1:], jnp.full((1,), n_blk, jnp.int32)]), reverse=True)
    next_exp = jnp.where(next_start < n_blk, blk_exp[jnp.minimum(next_start, n_blk - 1)], -1)

    def blk(i, nu):
        return jnp.maximum(jnp.minimum(i, nu[0] - 1), 0)

    def bias_map(i, be, nu, *_):
        return (layer, be[blk(i, nu)], 0, 0)

    in_specs = [
        pl.BlockSpec((MOE_BLOCK, d), lambda i, be, nu, *_: (blk(i, nu), 0)),
        pl.BlockSpec((1, 1, 1, 2 * D_EXPERT), bias_map),
        pl.BlockSpec((1, 1, 1, d), bias_map),
        pl.BlockSpec(memory_space=pl.ANY),
        pl.BlockSpec(memory_space=pl.ANY),
    ]
    depth = w1.shape[0]
    args = [blk_exp, n_used, first, slot.astype(jnp.int32), next_exp.astype(jnp.int32), xs,
            b1.reshape(depth, N_EXPERTS, 1, 2 * D_EXPERT), b2.reshape(depth, N_EXPERTS, 1, d), w1, w2]
    aliases = {}
    if prev is not None:
        in_specs.append(pl.BlockSpec(memory_space=pl.ANY))
        aliases = {len(args): 0}
        args.append(prev)
    grid_spec = pltpu.PrefetchScalarGridSpec(
        num_scalar_prefetch=5,
        grid=(n_blk,),
        in_specs=in_specs,
        out_specs=pl.BlockSpec((MOE_BLOCK, d), lambda i, be, nu, *_: (blk0 + blk(i, nu), 0)),
        scratch_shapes=[pltpu.VMEM((2, d, 2 * D_EXPERT), F32), pltpu.VMEM((2, D_EXPERT, d), F32),
                        pltpu.VMEM((d, 2 * D_EXPERT), BF16), pltpu.VMEM((D_EXPERT, d), BF16),
                        pltpu.SemaphoreType.DMA((2, 2))],
    )
    return pl.pallas_call(
        functools.partial(_moe_kernel, layer=layer),
        grid_spec=grid_spec,
        out_shape=jax.ShapeDtypeStruct((n_slot, d), BF16),
        input_output_aliases=aliases,
        compiler_params=pltpu.CompilerParams(dimension_semantics=("arbitrary",),
                                             vmem_limit_bytes=MOE_VMEM_LIMIT),
        name="moe_experts",
    )(*args)


def _combine_kernel(*refs, final):
    if final:
        x_ref, y_ref, g_ref, gate_ref, gf_ref, o_ref = refs
    else:
        x_ref, y_ref, g_ref, gate_ref, o_ref = refs
    g = g_ref[...]
    acc = y_ref[0].astype(F32) * g[:, 0:1]
    for k in range(1, TOP_K):
        acc = acc + y_ref[k].astype(F32) * g[:, k:k + 1]
    xn = x_ref[...] + gate_ref[0] * acc
    o_ref[...] = _rms(xn, gf_ref[...]) if final else xn


def _combine(x, yk, gates, mod, seg_of_tile, rows, final_g=None):
    d = x.shape[1]
    tm = _row_tile(rows)
    final = final_g is not None
    in_specs = [pl.BlockSpec((tm, d), lambda i: (i, 0)),
                pl.BlockSpec((TOP_K, tm, d), lambda i: (0, i, 0)),
                pl.BlockSpec((tm, TOP_K), lambda i: (i, 0)),
                pl.BlockSpec((1, 1, d), lambda i: (seg_of_tile(i, tm) * 6 + 5, 0, 0))]
    args = [x, yk, gates, mod]
    if final:
        in_specs.append(pl.BlockSpec((1, d), lambda i: (0, 0)))
        args.append(final_g.reshape(1, d))
    return pl.pallas_call(
        functools.partial(_combine_kernel, final=final),
        grid=(rows // tm,),
        in_specs=in_specs,
        out_specs=pl.BlockSpec((tm, d), lambda i: (i, 0)),
        out_shape=jax.ShapeDtypeStruct((rows, d), F32),
        compiler_params=_cparams("parallel"),
        name="moe_combine",
    )(*args)


def _moe(h, top_idx, rank, counts, layer, w1, b1, w2, b2):
    t, d = h.shape
    n_asg = t * TOP_K
    padded = (counts + MOE_BLOCK - 1) // MOE_BLOCK * MOE_BLOCK
    pad_end = jnp.cumsum(padded)
    pad_start = pad_end - padded
    grp_start = jnp.cumsum(counts) - counts
    experts = jnp.arange(N_EXPERTS, dtype=jnp.int32)
    dest = rank + jnp.sum(jnp.where(top_idx[:, :, None] == experts, pad_start, 0), axis=-1)
    dest_flat = dest.reshape(n_asg)
    tok_flat = jnp.tile(jnp.arange(t, dtype=jnp.int32), TOP_K)
    _, tok_sorted = lax.sort((dest_flat, tok_flat), num_keys=1)
    n_blk = (n_asg + N_EXPERTS * (MOE_BLOCK - 1) + MOE_BLOCK - 1) // MOE_BLOCK
    n_blk = -(-n_blk // MOE_PARTS) * MOE_PARTS
    blk_start = jnp.arange(n_blk, dtype=jnp.int32) * MOE_BLOCK
    blk_exp = jnp.minimum(jnp.sum((pad_end[None, :] <= blk_start[:, None]).astype(jnp.int32), axis=1),
                          N_EXPERTS - 1)
    n_used = (pad_end[-1:] // MOE_BLOCK).astype(jnp.int32)
    shift = jnp.repeat((grp_start - pad_start)[blk_exp], MOE_BLOCK)
    src = jnp.clip(jnp.arange(n_blk * MOE_BLOCK, dtype=jnp.int32) + shift, 0, n_asg - 1)
    buf_tok = tok_sorted[src]
    per = n_blk // MOE_PARTS
    ys = None
    for part in range(MOE_PARTS):
        lo = part * per
        xs = h[buf_tok[lo * MOE_BLOCK:(lo + per) * MOE_BLOCK]]
        ys = _moe_experts(xs, blk_exp[lo:lo + per], jnp.clip(n_used - lo, 0, per), layer, w1, b1, w2, b2,
                          n_blk * MOE_BLOCK, lo, ys)
    return ys[dest_flat].reshape(TOP_K, t, d)


def _pad_cols(w, n):
    return jnp.pad(w, ((0, 0), (0, n - w.shape[1])))


A_HEAD_ORDER = tuple(h for blk in range(A_HEADS // 2) for h in (blk, blk + A_HEADS // 2))


def _even_layer_attn(x, mod, seg_of_tile, norm1_g, w_in, sink, q_norm_g, w_uq, kv_norm_g, w_ukv, w_out,
                     tables, bsz, s_len, n_ctx):
    d = x.shape[1]
    proj = _even_proj(x, mod, seg_of_tile, norm1_g, w_in, q_norm_g, w_uq, kv_norm_g, w_ukv, tables, bsz, s_len)
    n_blk = A_HEADS // 2
    q_cb = lambda h: EV_QA // LANES + h
    k_cb = lambda h: EV_KA // LANES
    v_cb = lambda h: EV_VA // LANES
    oa_l = _pair_attn(proj, q_cb, proj, k_cb, proj, v_cb, n_blk, bsz, s_len, n_ctx, sink, window=True)
    oa_c = _pair_attn(proj, q_cb, proj, k_cb, proj, v_cb, n_blk, bsz, s_len, n_ctx, sink, window=False)
    ob_l = _mla_attn(proj, EV_QB, EV_KB, EV_VB, bsz, s_len, n_ctx, latent=True)
    ob_c = _mla_attn(proj, EV_QB, EV_KB, EV_VB, bsz, s_len, n_ctx, latent=False)
    n_a = A_HEADS * HEAD_DIM
    w_oa = w_out[:n_a].reshape(A_HEADS, HEAD_DIM, d)[A_HEAD_ORDER, :, :].reshape(n_a, d)
    return [(oa_l, oa_c), (ob_l, ob_c)], [w_oa.astype(BF16), w_out[n_a:].astype(BF16)]


def _odd_layer_attn(x, mod, seg_of_tile, norm1_g, w_in, rpb, w_out, bsz, s_len, n_ctx):
    d = x.shape[1]
    width = C_HEADS * HEAD_DIM
    w1 = jnp.concatenate([w_in[:, :width] * (HEAD_DIM ** -0.5 * LOG2_E), w_in[:, width:]], axis=1)
    qkv = _norm_matmul(x, 0, d, norm1_g, w1.astype(BF16), mod, 0, 1, seg_of_tile)
    n_pair = C_HEADS // 2
    o_l = _na_attn(qkv, rpb, bsz, s_len, n_ctx)
    o_c = _pair_attn(qkv, lambda h: h, qkv, lambda h: n_pair + h, qkv, lambda h: 2 * n_pair + h,
                     n_pair, bsz, s_len, n_ctx, None, window=False)
    return [(o_l, o_c)], [w_out.astype(BF16)]


def kernel(x, c, ctx, c_ctx, ada_w, ada_b, norm1_g, norm2_g, ev_w_in, ev_sink, ev_q_norm_g, ev_w_uq,
           ev_kv_norm_g, ev_w_ukv, ev_w_out, od_w_in, od_rpb, od_w_out, router_w, router_b,
           exp_w1, exp_b1, exp_w2, exp_b2, final_g):
    bsz, s_len, d = x.shape
    n_ctx = ctx.shape[1]
    depth = ada_w.shape[0]
    n_lat = bsz * s_len
    assert bsz < MOD_ROWS and s_len % 512 == 0 and n_lat % n_ctx == 0

    def seg_of_tile(i, tm):
        return jnp.minimum(i * tm // s_len, bsz)

    c_rows = jnp.concatenate([c, c_ctx[None, :], jnp.zeros((MOD_ROWS - bsz - 1, d), F32)], axis=0)
    mods = _ada_modulation(c_rows, ada_w, ada_b)

    t = n_lat + bsz * n_ctx
    tables = _rope_tables(s_len, _row_tile(t))
    xs = jnp.concatenate([x.reshape(n_lat, d), ctx.reshape(bsz * n_ctx, d)], axis=0)
    for layer in range(depth):
        i = layer // 2
        last = layer == depth - 1
        mod = mods[layer].reshape(MOD_ROWS * 6, 1, d)
        if layer % 2 == 0:
            a_list, w_list = _even_layer_attn(xs, mod, seg_of_tile, norm1_g[layer], ev_w_in[i], ev_sink[i],
                                              ev_q_norm_g[i], ev_w_uq[i], ev_kv_norm_g[i], ev_w_ukv[i],
                                              ev_w_out[i], tables, bsz, s_len, n_ctx)
        else:
            a_list, w_list = _odd_layer_attn(xs, mod, seg_of_tile, norm1_g[layer], od_w_in[i], od_rpb[i],
                                             od_w_out[i], bsz, s_len, n_ctx)
        xs, h2, top_idx, top_gate, rank, counts = _out_router(a_list, w_list, xs, mod, norm2_g[layer],
                                                              router_w[layer], router_b[layer], seg_of_tile)
        yk = _moe(h2, top_idx, rank, counts[:, 0].astype(jnp.int32), layer, exp_w1, exp_b1, exp_w2, exp_b2)
        xs = _combine(xs, yk, top_gate.T, mod, seg_of_tile, n_lat if last else t, final_g if last else None)
    return xs.reshape(bsz, s_len, d)
```

```python
import functools

import numpy as np
import jax
import jax.numpy as jnp
from jax import lax
from jax.experimental import pallas as pl
from jax.experimental.pallas import tpu as pltpu

GRID_W = 64
HEAD_DIM = 64
ROPE_BASE = 10000.0
NORM_EPS = 1e-6
NEG_INF = -1e30

A_HEADS = 8
A_KV_HEADS = 2
A_WINDOW = 128
B_HEADS = 8
B_NOPE = 64
B_ROPE = 32
B_V = 64
B_Q_RANK = 768
B_KV_RANK = 256
C_HEADS = 16
NA_ROWS = 8
NA_COLS = 16

N_EXPERTS = 32
TOP_K = 4
D_EXPERT = 1024
SWIGLU_LIMIT = 7.0
SWIGLU_ALPHA = 1.702

LANES = 128
MOD_ROWS = 8
VMEM_LIMIT = 48 * 1024 * 1024
MOE_VMEM_LIMIT = 56 * 1024 * 1024

BF16 = jnp.bfloat16
F32 = jnp.float32
NT_DIMS = (((1,), (1,)), ((), ()))
LOG2_E = 1.4426950408889634


def _cparams(*sem):
    return pltpu.CompilerParams(dimension_semantics=sem, vmem_limit_bytes=VMEM_LIMIT)


def _dot(a, b):
    return jnp.dot(a, b, preferred_element_type=F32)


def _dot_nt(a, b):
    return lax.dot_general(a, b, NT_DIMS, preferred_element_type=F32)


def _ada_kernel(c_ref, w_ref, b_ref, o_ref):
    c = c_ref[...]
    s = c / (1.0 + jnp.exp(-c))
    o_ref[0] = jnp.dot(s, w_ref[0], preferred_element_type=F32, precision=lax.Precision.HIGHEST) + b_ref[0]


def _ada_modulation(c_rows, ada_w, ada_b):
    depth, d, n = ada_w.shape
    tn = 1024
    return pl.pallas_call(
        _ada_kernel,
        grid=(depth, n // tn),
        in_specs=[
            pl.BlockSpec((MOD_ROWS, d), lambda l, j: (0, 0)),
            pl.BlockSpec((1, d, tn), lambda l, j: (l, 0, j)),
            pl.BlockSpec((1, 1, tn), lambda l, j: (l, 0, j)),
        ],
        out_specs=pl.BlockSpec((1, MOD_ROWS, tn), lambda l, j: (l, 0, j)),
        out_shape=jax.ShapeDtypeStruct((depth, MOD_ROWS, n), F32),
        compiler_params=_cparams("parallel", "parallel"),
        name="ada_modulation",
    )(c_rows, ada_w, ada_b.reshape(depth, 1, n))


def _norm_mm_kernel(*refs, modulate):
    if modulate:
        x_ref, g_ref, sh_ref, sc_ref, w_ref, o_ref = refs
    else:
        x_ref, g_ref, w_ref, o_ref = refs
    x = x_ref[...].astype(F32)
    ms = jnp.mean(x * x, axis=-1, keepdims=True)
    h = x * lax.rsqrt(ms + NORM_EPS) * g_ref[...]
    if modulate:
        h = h * (1.0 + sc_ref[0]) + sh_ref[0]
    o_ref[...] = _dot(h.astype(BF16), w_ref[...]).astype(o_ref.dtype)


def _row_tile(t):
    return 512 if t % 512 == 0 else 256


def _norm_matmul(x, col_block, kdim, g, w, mod=None, shift_idx=0, scale_idx=0, seg_of_tile=None):
    t = x.shape[0]
    n = w.shape[1]
    tm = _row_tile(t)
    in_specs = [pl.BlockSpec((tm, kdim), lambda i: (i, col_block)),
                pl.BlockSpec((1, kdim), lambda i: (0, 0))]
    args = [x, g.reshape(1, kdim).astype(F32)]
    if mod is not None:
        in_specs += [pl.BlockSpec((1, 1, kdim), lambda i: (seg_of_tile(i, tm) * 6 + shift_idx, 0, 0)),
                     pl.BlockSpec((1, 1, kdim), lambda i: (seg_of_tile(i, tm) * 6 + scale_idx, 0, 0))]
        args += [mod, mod]
    in_specs.append(pl.BlockSpec((kdim, n), lambda i: (0, 0)))
    args.append(w)
    return pl.pallas_call(
        functools.partial(_norm_mm_kernel, modulate=mod is not None),
        grid=(t // tm,),
        in_specs=in_specs,
        out_specs=pl.BlockSpec((tm, n), lambda i: (i, 0)),
        out_shape=jax.ShapeDtypeStruct((t, n), BF16),
        compiler_params=_cparams("parallel"),
        name="norm_matmul",
    )(*args)


EV_QA, EV_KA, EV_VA = 0, 512, 640
EV_QB, EV_KB, EV_VB, EV_OUT = 768, 1792, 2816, 3328
P_QA, P_QA_ROT, P_KA, P_KA_ROT, P_VA, P_CQ, P_CKV, P_KPE, P_KPE_ROT, P_END = (
    0, 512, 1024, 1152, 1280, 1408, 2176, 2432, 2560, 2688)
TAB_COS64, TAB_SIN64, TAB_COSQ, TAB_SINQ, TAB_COSK, TAB_SINK, TAB_END = 0, 128, 256, 384, 512, 640, 768


def _rms(x, g):
    return x * lax.rsqrt(jnp.mean(x * x, axis=-1, keepdims=True) + NORM_EPS) * g


def _even_proj_kernel(x_ref, g_ref, sh_ref, sc_ref, w1_ref, tab_ref, gq_ref, wq_ref, gkv_ref, wk_ref, o_ref):
    h = _rms(x_ref[...], g_ref[...]) * (1.0 + sc_ref[0]) + sh_ref[0]
    p = _dot(h.astype(BF16), w1_ref[...])
    cos64, sin64 = tab_ref[:, TAB_COS64:TAB_SIN64], tab_ref[:, TAB_SIN64:TAB_COSQ]
    for b in range(A_HEADS // 2 + 1):
        lo = P_QA + LANES * b if b < A_HEADS // 2 else P_KA
        rot = P_QA_ROT + LANES * b if b < A_HEADS // 2 else P_KA_ROT
        o_ref[:, EV_QA + LANES * b:EV_QA + LANES * (b + 1)] = (
            p[:, lo:lo + LANES] * cos64 + p[:, rot:rot + LANES] * sin64).astype(BF16)
    o_ref[:, EV_VA:EV_QB] = p[:, P_VA:P_CQ].astype(BF16)

    cq = _rms(p[:, P_CQ:P_CKV], gq_ref[...]).astype(BF16)
    q2 = _dot(cq, wq_ref[...])
    cosq, sinq = tab_ref[:, TAB_COSQ:TAB_SINQ], tab_ref[:, TAB_SINQ:TAB_COSK]
    n_q = B_HEADS * LANES
    for hd in range(B_HEADS):
        lo = LANES * hd
        o_ref[:, EV_QB + lo:EV_QB + lo + LANES] = (
            q2[:, lo:lo + LANES] * cosq + q2[:, n_q + lo:n_q + lo + LANES] * sinq).astype(BF16)

    ckv = _rms(p[:, P_CKV:P_KPE], gkv_ref[...]).astype(BF16)
    kpe = (p[:, P_KPE:P_KPE_ROT] * tab_ref[:, TAB_COSK:TAB_SINK]
           + p[:, P_KPE_ROT:P_END] * tab_ref[:, TAB_SINK:TAB_END]).astype(BF16)
    o_ref[:, EV_KB:EV_OUT] = _dot(jnp.concatenate([ckv, kpe], axis=1), wk_ref[...]).astype(BF16)


def _rot_cols(w, half):
    g = w.reshape(w.shape[:-1] + (w.shape[-1] // (2 * half), 2, half))
    return jnp.stack([-g[..., 1, :], g[..., 0, :]], axis=-2).reshape(w.shape)


def _rope_tables(s_len, tm):
    t = jnp.arange(s_len, dtype=jnp.int32)
    pos = ((t // GRID_W).astype(F32)[:, None], (t % GRID_W).astype(F32)[:, None])

    def pattern(n):
        half = n // 4
        inv_freq = ROPE_BASE ** (-(jnp.arange(half, dtype=F32) / half))
        ang = jnp.concatenate([pos[0] * inv_freq, pos[0] * inv_freq, pos[1] * inv_freq, pos[1] * inv_freq], axis=1)
        return jnp.cos(ang), jnp.sin(ang)

    c64, s64 = pattern(HEAD_DIM)
    c32, s32 = pattern(B_ROPE)
    one = lambda n: jnp.ones((s_len, n), F32)
    zero = lambda n: jnp.zeros((s_len, n), F32)
    rest = LANES - B_NOPE - B_ROPE
    tab = jnp.concatenate([
        c64, c64, s64, s64,
        one(B_NOPE), c32, one(rest), zero(B_NOPE), s32, zero(rest),
        c32, one(LANES - B_ROPE), s32, zero(LANES - B_ROPE)], axis=1)
    ident = jnp.concatenate([jnp.ones((tm, LANES), F32), jnp.zeros((tm, LANES), F32)] * 3, axis=1)
    return jnp.concatenate([tab, ident], axis=0)


def _even_proj(x, mod, seg_of_tile, norm1_g, w_in, q_norm_g, w_uq, kv_norm_g, w_ukv, tables, bsz, s_len):
    t, d = x.shape
    tm = _row_tile(t)
    sizes = (A_HEADS * HEAD_DIM, A_KV_HEADS * HEAD_DIM, A_KV_HEADS * HEAD_DIM, B_Q_RANK, B_KV_RANK, B_ROPE)
    offs = np.cumsum((0,) + sizes)
    part = [w_in[:, offs[k]:offs[k + 1]] for k in range(6)]
    w_qa = part[0].reshape(d, A_HEADS, HEAD_DIM)[:, A_HEAD_ORDER, :].reshape(d, -1) * (HEAD_DIM ** -0.5 * LOG2_E)
    pad_blk = lambda w: _pad_cols(w, LANES)
    w1 = jnp.concatenate([w_qa, _rot_cols(w_qa, HEAD_DIM // 4), part[1], _rot_cols(part[1], HEAD_DIM // 4), part[2],
                          part[3], part[4], pad_blk(part[5]), pad_blk(_rot_cols(part[5], B_ROPE // 4))],
                         axis=1).astype(BF16)
    assert w1.shape[1] == P_END
    qk_dim = B_NOPE + B_ROPE
    rest = LANES - qk_dim
    wq = w_uq.reshape(B_Q_RANK, B_HEADS, qk_dim) * (qk_dim ** -0.5 * LOG2_E)
    wq_plain = jnp.pad(wq, ((0, 0), (0, 0), (0, rest)))
    wq_rot = jnp.pad(_rot_cols(wq[..., B_NOPE:], B_ROPE // 4), ((0, 0), (0, 0), (B_NOPE, rest)))
    wq2 = jnp.concatenate([wq_plain.reshape(B_Q_RANK, -1), wq_rot.reshape(B_Q_RANK, -1)], axis=1).astype(BF16)
    wkv = w_ukv.reshape(B_KV_RANK, B_HEADS, B_NOPE + B_V)
    wk = jnp.pad(wkv[..., :B_NOPE], ((0, 0), (0, 0), (0, LANES - B_NOPE))).reshape(B_KV_RANK, -1)
    place = jnp.pad(jnp.eye(B_ROPE, dtype=F32), ((0, LANES - B_ROPE), (B_NOPE, rest)))
    wk2 = jnp.concatenate([
        jnp.concatenate([wk, wkv[..., B_NOPE:].reshape(B_KV_RANK, -1)], axis=1),
        jnp.concatenate([jnp.tile(place, (1, B_HEADS)), jnp.zeros((LANES, B_HEADS * B_V), F32)], axis=1)],
        axis=0).astype(BF16)
    n_lat_tiles = bsz * s_len // tm
    per_batch = s_len // tm

    def tab_map(i):
        return (jnp.where(i < n_lat_tiles, i % per_batch, per_batch), 0)

    const = lambda i: (0, 0)
    return pl.pallas_call(
        _even_proj_kernel,
        grid=(t // tm,),
        in_specs=[
            pl.BlockSpec((tm, d), lambda i: (i, 0)),
            pl.BlockSpec((1, d), const),
            pl.BlockSpec((1, 1, d), lambda i: (seg_of_tile(i, tm) * 6 + 0, 0, 0)),
            pl.BlockSpec((1, 1, d), lambda i: (seg_of_tile(i, tm) * 6 + 1, 0, 0)),
            pl.BlockSpec(w1.shape, const),
            pl.BlockSpec((tm, TAB_END), tab_map),
            pl.BlockSpec((1, B_Q_RANK), const),
            pl.BlockSpec(wq2.shape, const),
            pl.BlockSpec((1, B_KV_RANK), const),
            pl.BlockSpec(wk2.shape, const),
        ],
        out_specs=pl.BlockSpec((tm, EV_OUT), lambda i: (i, 0)),
        out_shape=jax.ShapeDtypeStruct((t, EV_OUT), BF16),
        compiler_params=pltpu.CompilerParams(dimension_semantics=("parallel",), vmem_limit_bytes=MOE_VMEM_LIMIT),
        name="even_proj",
    )(x, norm1_g.reshape(1, d), mod, mod, w1, tables, q_norm_g.reshape(1, -1), wq2,
      kv_norm_g.reshape(1, -1), wk2)


def _out_router_kernel(*refs, n_a, n_lat_tiles):
    al_refs = refs[:n_a]
    ac_refs = refs[n_a:2 * n_a]
    w_refs = refs[2 * n_a:3 * n_a]
    x_ref, gate_ref, g2_ref, sh_ref, sc_ref, rwt_ref, rb_ref, tri_ref = refs[3 * n_a:3 * n_a + 8]
    xo_ref, h_ref, ti_ref, tg_ref, rk_ref, cnt_ref, cnt_scr = refs[3 * n_a + 8:]
    is_latent = pl.program_id(0) < n_lat_tiles
    acc = None
    for k in range(n_a):
        a = jnp.where(is_latent, al_refs[k][...], ac_refs[k][...])
        part = _dot(a, w_refs[k][...])
        acc = part if acc is None else acc + part
    xn = x_ref[...] + gate_ref[0] * acc
    xo_ref[...] = xn
    ms = jnp.mean(xn * xn, axis=-1, keepdims=True)
    h = xn * lax.rsqrt(ms + NORM_EPS) * g2_ref[...]
    h = h * (1.0 + sc_ref[0]) + sh_ref[0]
    h_ref[...] = h.astype(BF16)
    h_hi = h.astype(BF16)
    h_lo = (h - h_hi.astype(F32)).astype(BF16)
    both = _dot_nt(rwt_ref[...], h_hi)
    logits = (both[:N_EXPERTS] + both[N_EXPERTS:]) + _dot_nt(rwt_ref[:N_EXPERTS, :], h_lo) + rb_ref[...]
    eidx = lax.broadcasted_iota(jnp.int32, logits.shape, 0)
    vals, idxs, hots = [], [], []
    cur = logits
    for _ in range(TOP_K):
        m = jnp.max(cur, axis=0, keepdims=True)
        idx = jnp.min(jnp.where(cur == m, eidx, N_EXPERTS), axis=0, keepdims=True)
        hot = eidx == idx
        vals.append(m)
        idxs.append(idx)
        hots.append(hot)
        cur = jnp.where(hot, -jnp.inf, cur)
    es = [jnp.exp(v - vals[0]) for v in vals]
    den = es[0] + es[1] + es[2] + es[3]
    ti_ref[...] = jnp.concatenate(idxs, axis=0)
    tg_ref[...] = jnp.concatenate([e / den for e in es], axis=0)

    @pl.when(pl.program_id(0) == 0)
    def _():
        cnt_scr[...] = jnp.zeros_like(cnt_scr)

    chosen = jnp.where(hots[0] | hots[1] | hots[2] | hots[3], 1.0, 0.0)
    before = cnt_scr[...] + _dot(chosen.astype(BF16), tri_ref[...])
    rk_ref[...] = jnp.concatenate(
        [jnp.sum(jnp.where(hot, before, 0.0), axis=0, keepdims=True) for hot in hots], axis=0).astype(jnp.int32)
    total = cnt_scr[...] + jnp.sum(chosen, axis=1, keepdims=True)
    cnt_scr[...] = total
    cnt_ref[...] = jnp.broadcast_to(total, cnt_ref.shape)


def _out_router(a_list, w_list, x, mod, g2, router_w, router_b, seg_of_tile):
    t, d = x.shape
    tm = _row_tile(t)
    n_a = len(a_list)
    n_lat_tiles = a_list[0][0].shape[0] // tm

    def mod_spec(idx):
        return pl.BlockSpec((1, 1, d), lambda i: (seg_of_tile(i, tm) * 6 + idx, 0, 0))

    in_specs = [pl.BlockSpec((tm, al.shape[1]), lambda i: (jnp.minimum(i, n_lat_tiles - 1), 0)) for al, _ in a_list]
    in_specs += [pl.BlockSpec((tm, ac.shape[1]), lambda i: (jnp.maximum(i - n_lat_tiles, 0), 0)) for _, ac in a_list]
    in_specs += [pl.BlockSpec(w.shape, lambda i: (0, 0)) for w in w_list]
    in_specs += [pl.BlockSpec((tm, d), lambda i: (i, 0)), mod_spec(2),
                 pl.BlockSpec((1, d), lambda i: (0, 0)), mod_spec(3), mod_spec(4),
                 pl.BlockSpec((2 * N_EXPERTS, d), lambda i: (0, 0)),
                 pl.BlockSpec((N_EXPERTS, 1), lambda i: (0, 0)),
                 pl.BlockSpec((tm, tm), lambda i: (0, 0))]
    rw_hi = router_w.T.astype(BF16)
    rw_lo = (router_w.T - rw_hi.astype(F32)).astype(BF16)
    strictly_upper = jnp.asarray(np.triu(np.ones((tm, tm), np.float32), k=1), BF16)
    return pl.pallas_call(
        functools.partial(_out_router_kernel, n_a=n_a, n_lat_tiles=n_lat_tiles),
        grid=(t // tm,),
        in_specs=in_specs,
        out_specs=[pl.BlockSpec((tm, d), lambda i: (i, 0)),
                   pl.BlockSpec((tm, d), lambda i: (i, 0)),
                   pl.BlockSpec((TOP_K, tm), lambda i: (0, i)),
                   pl.BlockSpec((TOP_K, tm), lambda i: (0, i)),
                   pl.BlockSpec((TOP_K, tm), lambda i: (0, i)),
                   pl.BlockSpec((N_EXPERTS, LANES), lambda i: (0, 0))],
        out_shape=[jax.ShapeDtypeStruct((t, d), F32),
                   jax.ShapeDtypeStruct((t, d), BF16),
                   jax.ShapeDtypeStruct((TOP_K, t), jnp.int32),
                   jax.ShapeDtypeStruct((TOP_K, t), F32),
                   jax.ShapeDtypeStruct((TOP_K, t), jnp.int32),
                   jax.ShapeDtypeStruct((N_EXPERTS, LANES), F32)],
        scratch_shapes=[pltpu.VMEM((N_EXPERTS, 1), F32)],
        compiler_params=_cparams("arbitrary"),
        name="out_router",
    )(*[al for al, _ in a_list], *[ac for _, ac in a_list], *w_list, x, mod, g2.reshape(1, d), mod, mod,
      jnp.concatenate([rw_hi, rw_lo], axis=0), router_b.reshape(N_EXPERTS, 1), strictly_upper)


def _half_mask(shape, j):
    lane = lax.broadcasted_iota(jnp.int32, shape, 1)
    return (lane >= HEAD_DIM * j) & (lane < HEAD_DIM * (j + 1))


def _softmax_pv(scores, values, sink=None):
    m = jnp.max(scores[0], axis=1, keepdims=True)
    for s in scores[1:]:
        m = jnp.maximum(m, jnp.max(s, axis=1, keepdims=True))
    if sink is not None:
        m = jnp.maximum(m, sink)
    den = None
    out = None
    for s, v in zip(scores, values):
        p = jnp.exp2(s - m)
        ps = jnp.sum(p, axis=1, keepdims=True)
        den = ps if den is None else den + ps
        o = _dot(p.astype(BF16), v)
        out = o if out is None else out + o
    if sink is not None:
        den = den + jnp.exp2(sink - m)
    return out / den


WINDOW_SUB = 256


def _pair_attn_kernel(*refs, window, tq, s_len, has_sink, heads_per_group):
    if window:
        q_ref, kl_ref, vl_ref, kc_ref, vc_ref = refs[:5]
        rest = refs[5:]
    else:
        q_ref, kc_ref, vc_ref = refs[:3]
        rest = refs[3:]
    if has_sink:
        sink_ref, o_ref = rest
    else:
        (o_ref,) = rest
    blk = pl.program_id(1)
    kc = kc_ref[...]
    vc = vc_ref[...]
    sub = min(tq, WINDOW_SUB) if window else tq
    for u in range(tq // sub):
        q = q_ref[u * sub:(u + 1) * sub, :]
        if window:
            wl = sub + 2 * A_WINDOW
            q0 = pl.program_id(2) * tq + u * sub
            start = pl.multiple_of(jnp.clip(q0 - A_WINDOW, 0, s_len - wl), LANES)
            kw = kl_ref[pl.ds(start, wl), :]
            vw = vl_ref[pl.ds(start, wl), :]
            qpos = q0 + lax.broadcasted_iota(jnp.int32, (sub, wl), 0)
            kpos = start + lax.broadcasted_iota(jnp.int32, (sub, wl), 1)
            band = jnp.abs(qpos - kpos) <= A_WINDOW
        outs = []
        for j in range(2):
            qj = jnp.where(_half_mask(q.shape, j), q, jnp.zeros_like(q))
            scores, values = [], []
            if window:
                scores.append(jnp.where(band, _dot_nt(qj, kw), NEG_INF))
                values.append(vw)
            scores.append(_dot_nt(qj, kc))
            values.append(vc)
            sink = sink_ref[j * heads_per_group + blk] if has_sink else None
            outs.append(_softmax_pv(scores, values, sink))
        o_ref[u * sub:(u + 1) * sub, :] = jnp.where(_half_mask(outs[0].shape, 0), outs[0], outs[1]).astype(o_ref.dtype)


def _pair_attn(q_arr, q_cb, k_arr, k_cb, v_arr, v_cb, n_blk, bsz, s_len, n_ctx, sink, window):
    ctx_blk0 = bsz * s_len // n_ctx
    if window:
        tq = min(1024, s_len)
        nq = s_len // tq
        q_spec = pl.BlockSpec((tq, LANES), lambda b, h, i: (b * nq + i, q_cb(h)))
        kv_specs = [pl.BlockSpec((s_len, LANES), lambda b, h, i: (b, k_cb(h))),
                    pl.BlockSpec((s_len, LANES), lambda b, h, i: (b, v_cb(h)))]
        args = [q_arr, k_arr, v_arr, k_arr, v_arr]
        out_rows = bsz * s_len
    else:
        tq = n_ctx
        nq = 1
        q_spec = pl.BlockSpec((tq, LANES), lambda b, h, i: (ctx_blk0 + b, q_cb(h)))
        kv_specs = []
        args = [q_arr, k_arr, v_arr]
        out_rows = bsz * n_ctx
    kv_specs += [pl.BlockSpec((n_ctx, LANES), lambda b, h, i: (ctx_blk0 + b, k_cb(h))),
                 pl.BlockSpec((n_ctx, LANES), lambda b, h, i: (ctx_blk0 + b, v_cb(h)))]
    in_specs = [q_spec] + kv_specs
    if sink is not None:
        in_specs.append(pl.BlockSpec(memory_space=pltpu.SMEM))
        args.append(sink.astype(F32) * LOG2_E)
    return pl.pallas_call(
        functools.partial(_pair_attn_kernel, window=window, tq=tq, s_len=s_len,
                          has_sink=sink is not None, heads_per_group=n_blk),
        grid=(bsz, n_blk, nq),
        in_specs=in_specs,
        out_specs=pl.BlockSpec((tq, LANES), lambda b, h, i: (b * nq + i, h)),
        out_shape=jax.ShapeDtypeStruct((out_rows, n_blk * LANES), BF16),
        compiler_params=_cparams("parallel", "parallel", "arbitrary"),
        name="pair_attn_window" if window else "pair_attn_ctx",
    )(*args)


def _mla_kernel(*refs, latent, tk, n_chunks, sub):
    if latent:
        q_ref, kl_ref, vl_ref, kc_ref, vc_ref, o_ref = refs
    else:
        q_ref, kc_ref, vc_ref, o_ref = refs
    streams = [(j, r) for j in range(2) for r in range(q_ref.shape[0] // sub)]

    def update(j, r, k, v, state):
        m, l, acc = state
        s = _dot_nt(q_ref[r * sub:(r + 1) * sub, LANES * j:LANES * (j + 1)], k)
        m_new = jnp.maximum(m, jnp.max(s, axis=1, keepdims=True))
        p = jnp.exp2(s - m_new)
        alpha = jnp.exp2(m - m_new)
        l_new = alpha * l + jnp.sum(p, axis=1, keepdims=True)
        acc_new = alpha * acc + _dot(p.astype(BF16), v)
        return m_new, l_new, acc_new

    init = (jnp.full((sub, 1), NEG_INF, F32), jnp.zeros((sub, 1), F32), jnp.zeros((sub, LANES), F32))
    vc = vc_ref[...]
    state = tuple(update(j, r, kc_ref[:, LANES * j:LANES * (j + 1)], vc, init) for j, r in streams)
    if latent:
        for c in range(n_chunks):
            v = vl_ref[c * tk:(c + 1) * tk, :]
            state = tuple(update(j, r, kl_ref[c * tk:(c + 1) * tk, LANES * j:LANES * (j + 1)], v, st)
                          for (j, r), st in zip(streams, state))
    for r in range(q_ref.shape[0] // sub):
        o0, o1 = [state[streams.index((j, r))] for j in range(2)]
        o0 = o0[2] / o0[1]
        o1 = o1[2] / o1[1]
        o_ref[r * sub:(r + 1) * sub, :] = jnp.where(_half_mask(o0.shape, 0), o0, o1).astype(o_ref.dtype)


def _mla_attn(arr, q_col, k_col, v_col, bsz, s_len, n_ctx, latent):
    n_pair = B_HEADS // 2
    ctx_blk0 = bsz * s_len // n_ctx
    qc, kc, vc = q_col // (2 * LANES), k_col // (2 * LANES), v_col // LANES
    kv_specs = [pl.BlockSpec((n_ctx, 2 * LANES), lambda b, h, i: (ctx_blk0 + b, kc + h)),
                pl.BlockSpec((n_ctx, LANES), lambda b, h, i: (ctx_blk0 + b, vc + h))]
    if latent:
        tq, tk = 512, min(2048, s_len)
        nq = s_len // tq
        q_spec = pl.BlockSpec((tq, 2 * LANES), lambda b, h, i: (b * nq + i, qc + h))
        kv_specs = [pl.BlockSpec((s_len, 2 * LANES), lambda b, h, i: (b, kc + h)),
                    pl.BlockSpec((s_len, LANES), lambda b, h, i: (b, vc + h))] + kv_specs
        args = [arr] * 5
        out_rows = bsz * s_len
    else:
        tq, tk = n_ctx, n_ctx
        nq = 1
        q_spec = pl.BlockSpec((tq, 2 * LANES), lambda b, h, i: (ctx_blk0 + b, qc + h))
        args = [arr] * 3
        out_rows = bsz * n_ctx
    return pl.pallas_call(
        functools.partial(_mla_kernel, latent=latent, tk=tk, n_chunks=s_len // tk, sub=min(tq, 512)),
        grid=(bsz, n_pair, nq),
        in_specs=[q_spec] + kv_specs,
        out_specs=pl.BlockSpec((tq, LANES), lambda b, h, i: (b * nq + i, h)),
        out_shape=jax.ShapeDtypeStruct((out_rows, n_pair * LANES), BF16),
        compiler_params=_cparams("parallel", "parallel", "arbitrary"),
        name="mla_latent" if latent else "mla_ctx",
    )(*args)


NA_Q_ROWS = 4
NA_K_ROWS = NA_Q_ROWS + NA_ROWS


NA_SUB = 4


def _na_kernel(pat_ref, start_ref, q_ref, kl_ref, vl_ref, kc_ref, vc_ref, *rest):
    del pat_ref
    bias_refs, o_ref = rest[:-1], rest[-1]
    rb = pl.program_id(2)
    nk = NA_K_ROWS * GRID_W
    tq = NA_Q_ROWS * GRID_W
    kc = kc_ref[...]
    vc = vc_ref[...]
    for u, bias_ref in enumerate(bias_refs):
        start = pl.multiple_of(start_ref[rb * len(bias_refs) + u] * GRID_W, NA_Q_ROWS * GRID_W)
        kw = kl_ref[pl.ds(start, nk), :]
        vw = vl_ref[pl.ds(start, nk), :]
        q = q_ref[u * tq:(u + 1) * tq, :]
        outs = []
        for j in range(2):
            qj = jnp.where(_half_mask(q.shape, j), q, jnp.zeros_like(q))
            sw = _dot_nt(qj, kw) + bias_ref[0, j]
            sc = _dot_nt(qj, kc)
            outs.append(_softmax_pv([sw, sc], [vw, vc]))
        o_ref[u * tq:(u + 1) * tq, :] = jnp.where(_half_mask(outs[0].shape, 0), outs[0], outs[1]).astype(o_ref.dtype)


def _na_patterns(n_rows):
    kh = NA_ROWS
    n_rb = n_rows // NA_Q_ROWS
    starts, keys = [], []
    for rb in range(n_rb):
        r_a = rb * NA_Q_ROWS
        start = int(np.clip(r_a - NA_Q_ROWS, 0, n_rows - NA_K_ROWS))
        assert start % NA_Q_ROWS == 0
        rows = r_a + np.arange(NA_Q_ROWS)
        r0 = np.clip(rows - kh // 2, 0, n_rows - kh)
        assert start <= r0.min() and r0.max() + kh <= start + NA_K_ROWS
        starts.append(start)
        keys.append((r_a - start, tuple((r0 - start).tolist())))
    uniq = sorted(set(keys))
    pat = [uniq.index(k) for k in keys]
    return np.asarray(starts, np.int32), np.asarray(pat, np.int32), uniq


def _na_bias_table(rpb, uniq):
    n_dc = 2 * NA_COLS - 1
    i = np.arange(NA_Q_ROWS)[:, None]
    j = np.arange(NA_K_ROWS)[None, :]
    c = np.arange(GRID_W)[:, None]
    kc = np.arange(GRID_W)[None, :]
    c0 = np.clip(c - NA_COLS // 2, 0, GRID_W - NA_COLS)
    cvalid = (kc >= c0) & (kc < c0 + NA_COLS)
    dc = np.clip(kc - c + NA_COLS - 1, 0, n_dc - 1)
    csel = np.eye(n_dc, dtype=np.float32)[dc] * cvalid[..., None]
    tiles = jnp.einsum('hab,cqb->hacq', rpb.astype(F32), jnp.asarray(csel), precision=lax.Precision.HIGHEST)
    tiles = jnp.where(jnp.asarray(cvalid), tiles * LOG2_E, NEG_INF)
    masked = jnp.full((C_HEADS, GRID_W, GRID_W), NEG_INF, F32)
    tables = []
    for delta, r0_rel in uniq:
        r0_rel = np.asarray(r0_rel)[:, None]
        valid = (j >= r0_rel) & (j < r0_rel + NA_ROWS)
        dr = j - (delta + i) + NA_ROWS - 1
        rows = [jnp.concatenate([tiles[:, dr[qi, kj]] if valid[qi, kj] else masked for kj in range(NA_K_ROWS)],
                                axis=-1) for qi in range(NA_Q_ROWS)]
        tables.append(jnp.concatenate(rows, axis=-2))
    return jnp.stack(tables)


def _na_attn(qkv, rpb, bsz, s_len, n_ctx):
    n_pair = C_HEADS // 2
    n_rows = s_len // GRID_W
    starts, pat, uniq = _na_patterns(n_rows)
    bias = _na_bias_table(rpb, uniq)
    n_sub = min(NA_SUB, n_rows // NA_Q_ROWS)
    n_rb = n_rows // NA_Q_ROWS // n_sub
    tq = NA_Q_ROWS * GRID_W * n_sub
    nk = NA_K_ROWS * GRID_W
    ctx_blk0 = bsz * s_len // n_ctx
    bias_specs = [pl.BlockSpec((1, 2, tq // n_sub, nk), lambda h, b, r, pat, st, u=u: (pat[r * n_sub + u], h, 0, 0))
                  for u in range(n_sub)]
    grid_spec = pltpu.PrefetchScalarGridSpec(
        num_scalar_prefetch=2,
        grid=(n_pair, bsz, n_rb),
        in_specs=[
            pl.BlockSpec((tq, LANES), lambda h, b, r, pat, st: (b * n_rb + r, h)),
            pl.BlockSpec((s_len, LANES), lambda h, b, r, pat, st: (b, n_pair + h)),
            pl.BlockSpec((s_len, LANES), lambda h, b, r, pat, st: (b, 2 * n_pair + h)),
            pl.BlockSpec((n_ctx, LANES), lambda h, b, r, pat, st: (ctx_blk0 + b, n_pair + h)),
            pl.BlockSpec((n_ctx, LANES), lambda h, b, r, pat, st: (ctx_blk0 + b, 2 * n_pair + h)),
        ] + bias_specs,
        out_specs=pl.BlockSpec((tq, LANES), lambda h, b, r, pat, st: (b * n_rb + r, h)),
    )
    return pl.pallas_call(
        _na_kernel,
        grid_spec=grid_spec,
        out_shape=jax.ShapeDtypeStruct((bsz * s_len, n_pair * LANES), BF16),
        compiler_params=_cparams("parallel", "parallel", "arbitrary"),
        name="na_latent",
    )(jnp.asarray(pat), jnp.asarray(starts), qkv, qkv, qkv, qkv, qkv, *([bias] * n_sub))


MOE_BLOCK = 512
MOE_FC = 512
MOE_PARTS = 4


def _moe_kernel(*refs, layer):
    be_ref, nu_ref, first_ref, slot_ref, next_ref, x_ref, b1_ref, b2_ref, w1_hbm, w2_hbm = refs[:10]
    o_ref, w1_f32, w2_f32, w1_bf, w2_bf, sem = refs[-6:]
    i = pl.program_id(0)

    def weight_copies(expert, slot):
        return (pltpu.make_async_copy(w1_hbm.at[layer, expert], w1_f32.at[slot], sem.at[0, slot]),
                pltpu.make_async_copy(w2_hbm.at[layer, expert], w2_f32.at[slot], sem.at[1, slot]))

    @pl.when(i < nu_ref[0])
    def _():
        @pl.when(first_ref[i] == 1)
        def _():
            slot = slot_ref[i]

            @pl.when(i == 0)
            def _():
                for cp in weight_copies(be_ref[i], slot):
                    cp.start()

            for cp in weight_copies(be_ref[i], slot):
                cp.wait()

            @pl.when(next_ref[i] >= 0)
            def _():
                for cp in weight_copies(next_ref[i], 1 - slot):
                    cp.start()

            w1_bf[...] = w1_f32[slot].astype(BF16)
            w2_bf[...] = w2_f32[slot].astype(BF16)

        x = x_ref[...]
        acc = None
        for c in range(D_EXPERT // MOE_FC):
            lo, hi = c * MOE_FC, (c + 1) * MOE_FC
            glu = _dot(x, w1_bf[:, lo:hi]) + b1_ref[0, 0, :, lo:hi]
            lin = _dot(x, w1_bf[:, D_EXPERT + lo:D_EXPERT + hi]) + b1_ref[0, 0, :, D_EXPERT + lo:D_EXPERT + hi]
            glu = jnp.minimum(glu, SWIGLU_LIMIT)
            lin = jnp.clip(lin, -SWIGLU_LIMIT, SWIGLU_LIMIT)
            act = glu * (1.0 / (1.0 + jnp.exp(-SWIGLU_ALPHA * glu))) * (lin + 1.0)
            y = _dot(act.astype(BF16), w2_bf[lo:hi, :])
            acc = y if acc is None else acc + y
        o_ref[...] = (acc + b2_ref[0, 0]).astype(o_ref.dtype)

    @pl.when((i == 0) & (nu_ref[0] <= 0))
    def _():
        o_ref[...] = jnp.zeros_like(o_ref)


def _moe_experts(xs, blk_exp, n_used, layer, w1, b1, w2, b2, n_slot, blk0, prev=None):
    d = xs.shape[1]
    n_blk = xs.shape[0] // MOE_BLOCK

    idx = jnp.arange(n_blk, dtype=jnp.int32)
    prev_exp = jnp.concatenate([jnp.full((1,), -1, jnp.int32), blk_exp[:-1]])
    first = (((idx == 0) | (blk_exp != prev_exp)) & (idx < n_used[0])).astype(jnp.int32)
    slot = (jnp.cumsum(first) - 1) % 2
    starts = jnp.where(first == 1, idx, n_blk)
    next_start = lax.cummin(jnp.concatenate([starts[1:], jnp.full((1,), n_blk, jnp.int32)]), reverse=True)
    next_exp = jnp.where(next_start < n_blk, blk_exp[jnp.minimum(next_start, n_blk - 1)], -1)

    def blk(i, nu):
        return jnp.maximum(jnp.minimum(i, nu[0] - 1), 0)

    def bias_map(i, be, nu, *_):
        return (layer, be[blk(i, nu)], 0, 0)

    in_specs = [
        pl.BlockSpec((MOE_BLOCK, d), lambda i, be, nu, *_: (blk(i, nu), 0)),
        pl.BlockSpec((1, 1, 1, 2 * D_EXPERT), bias_map),
        pl.BlockSpec((1, 1, 1, d), bias_map),
        pl.BlockSpec(memory_space=pl.ANY),
        pl.BlockSpec(memory_space=pl.ANY),
    ]
    depth = w1.shape[0]
    args = [blk_exp, n_used, first, slot.astype(jnp.int32), next_exp.astype(jnp.int32), xs,
            b1.reshape(depth, N_EXPERTS, 1, 2 * D_EXPERT), b2.reshape(depth, N_EXPERTS, 1, d), w1, w2]
    aliases = {}
    if prev is not None:
        in_specs.append(pl.BlockSpec(memory_space=pl.ANY))
        aliases = {len(args): 0}
        args.append(prev)
    grid_spec = pltpu.PrefetchScalarGridSpec(
        num_scalar_prefetch=5,
        grid=(n_blk,),
        in_specs=in_specs,
        out_specs=pl.BlockSpec((MOE_BLOCK, d), lambda i, be, nu, *_: (blk0 + blk(i, nu), 0)),
        scratch_shapes=[pltpu.VMEM((2, d, 2 * D_EXPERT), F32), pltpu.VMEM((2, D_EXPERT, d), F32),
                        pltpu.VMEM((d, 2 * D_EXPERT), BF16), pltpu.VMEM((D_EXPERT, d), BF16),
                        pltpu.SemaphoreType.DMA((2, 2))],
    )
    return pl.pallas_call(
        functools.partial(_moe_kernel, layer=layer),
        grid_spec=grid_spec,
        out_shape=jax.ShapeDtypeStruct((n_slot, d), BF16),
        input_output_aliases=aliases,
        compiler_params=pltpu.CompilerParams(dimension_semantics=("arbitrary",),
                                             vmem_limit_bytes=MOE_VMEM_LIMIT),
        name="moe_experts",
    )(*args)


def _combine_kernel(*refs, final):
    if final:
        x_ref, y_ref, g_ref, gate_ref, gf_ref, o_ref = refs
    else:
        x_ref, y_ref, g_ref, gate_ref, o_ref = refs
    g = g_ref[...]
    acc = y_ref[0].astype(F32) * g[:, 0:1]
    for k in range(1, TOP_K):
        acc = acc + y_ref[k].astype(F32) * g[:, k:k + 1]
    xn = x_ref[...] + gate_ref[0] * acc
    o_ref[...] = _rms(xn, gf_ref[...]) if final else xn


def _combine(x, yk, gates, mod, seg_of_tile, rows, final_g=None):
    d = x.shape[1]
    tm = _row_tile(rows)
    final = final_g is not None
    in_specs = [pl.BlockSpec((tm, d), lambda i: (i, 0)),
                pl.BlockSpec((TOP_K, tm, d), lambda i: (0, i, 0)),
                pl.BlockSpec((tm, TOP_K), lambda i: (i, 0)),
                pl.BlockSpec((1, 1, d), lambda i: (seg_of_tile(i, tm) * 6 + 5, 0, 0))]
    args = [x, yk, gates, mod]
    if final:
        in_specs.append(pl.BlockSpec((1, d), lambda i: (0, 0)))
        args.append(final_g.reshape(1, d))
    return pl.pallas_call(
        functools.partial(_combine_kernel, final=final),
        grid=(rows // tm,),
        in_specs=in_specs,
        out_specs=pl.BlockSpec((tm, d), lambda i: (i, 0)),
        out_shape=jax.ShapeDtypeStruct((rows, d), F32),
        compiler_params=_cparams("parallel"),
        name="moe_combine",
    )(*args)


def _moe(h, top_idx, rank, counts, layer, w1, b1, w2, b2):
    t, d = h.shape
    n_asg = t * TOP_K
    padded = (counts + MOE_BLOCK - 1) // MOE_BLOCK * MOE_BLOCK
    pad_end = jnp.cumsum(padded)
    pad_start = pad_end - padded
    grp_start = jnp.cumsum(counts) - counts
    experts = jnp.arange(N_EXPERTS, dtype=jnp.int32)
    dest = rank + jnp.sum(jnp.where(top_idx[:, :, None] == experts, pad_start, 0), axis=-1)
    dest_flat = dest.reshape(n_asg)
    tok_flat = jnp.tile(jnp.arange(t, dtype=jnp.int32), TOP_K)
    _, tok_sorted = lax.sort((dest_flat, tok_flat), num_keys=1)
    n_blk = (n_asg + N_EXPERTS * (MOE_BLOCK - 1) + MOE_BLOCK - 1) // MOE_BLOCK
    n_blk = -(-n_blk // MOE_PARTS) * MOE_PARTS
    blk_start = jnp.arange(n_blk, dtype=jnp.int32) * MOE_BLOCK
    blk_exp = jnp.minimum(jnp.sum((pad_end[None, :] <= blk_start[:, None]).astype(jnp.int32), axis=1),
                          N_EXPERTS - 1)
    n_used = (pad_end[-1:] // MOE_BLOCK).astype(jnp.int32)
    shift = jnp.repeat((grp_start - pad_start)[blk_exp], MOE_BLOCK)
    src = jnp.clip(jnp.arange(n_blk * MOE_BLOCK, dtype=jnp.int32) + shift, 0, n_asg - 1)
    buf_tok = tok_sorted[src]
    per = n_blk // MOE_PARTS
    ys = None
    for part in range(MOE_PARTS):
        lo = part * per
        xs = h[buf_tok[lo * MOE_BLOCK:(lo + per) * MOE_BLOCK]]
        ys = _moe_experts(xs, blk_exp[lo:lo + per], jnp.clip(n_used - lo, 0, per), layer, w1, b1, w2, b2,
                          n_blk * MOE_BLOCK, lo, ys)
    return ys[dest_flat].reshape(TOP_K, t, d)


def _pad_cols(w, n):
    return jnp.pad(w, ((0, 0), (0, n - w.shape[1])))


A_HEAD_ORDER = tuple(h for blk in range(A_HEADS // 2) for h in (blk, blk + A_HEADS // 2))


def _even_layer_attn(x, mod, seg_of_tile, norm1_g, w_in, sink, q_norm_g, w_uq, kv_norm_g, w_ukv, w_out,
                     tables, bsz, s_len, n_ctx):
    d = x.shape[1]
    proj = _even_proj(x, mod, seg_of_tile, norm1_g, w_in, q_norm_g, w_uq, kv_norm_g, w_ukv, tables, bsz, s_len)
    n_blk = A_HEADS // 2
    q_cb = lambda h: EV_QA // LANES + h
    k_cb = lambda h: EV_KA // LANES
    v_cb = lambda h: EV_VA // LANES
    oa_l = _pair_attn(proj, q_cb, proj, k_cb, proj, v_cb, n_blk, bsz, s_len, n_ctx, sink, window=True)
    oa_c = _pair_attn(proj, q_cb, proj, k_cb, proj, v_cb, n_blk, bsz, s_len, n_ctx, sink, window=False)
    ob_l = _mla_attn(proj, EV_QB, EV_KB, EV_VB, bsz, s_len, n_ctx, latent=True)
    ob_c = _mla_attn(proj, EV_QB, EV_KB, EV_VB, bsz, s_len, n_ctx, latent=False)
    n_a = A_HEADS * HEAD_DIM
    w_oa = w_out[:n_a].reshape(A_HEADS, HEAD_DIM, d)[A_HEAD_ORDER, :, :].reshape(n_a, d)
    return [(oa_l, oa_c), (ob_l, ob_c)], [w_oa.astype(BF16), w_out[n_a:].astype(BF16)]


def _odd_layer_attn(x, mod, seg_of_tile, norm1_g, w_in, rpb, w_out, bsz, s_len, n_ctx):
    d = x.shape[1]
    width = C_HEADS * HEAD_DIM
    w1 = jnp.concatenate([w_in[:, :width] * (HEAD_DIM ** -0.5 * LOG2_E), w_in[:, width:]], axis=1)
    qkv = _norm_matmul(x, 0, d, norm1_g, w1.astype(BF16), mod, 0, 1, seg_of_tile)
    n_pair = C_HEADS // 2
    o_l = _na_attn(qkv, rpb, bsz, s_len, n_ctx)
    o_c = _pair_attn(qkv, lambda h: h, qkv, lambda h: n_pair + h, qkv, lambda h: 2 * n_pair + h,
                     n_pair, bsz, s_len, n_ctx, None, window=False)
    return [(o_l, o_c)], [w_out.astype(BF16)]


def kernel(x, c, ctx, c_ctx, ada_w, ada_b, norm1_g, norm2_g, ev_w_in, ev_sink, ev_q_norm_g, ev_w_uq,
           ev_kv_norm_g, ev_w_ukv, ev_w_out, od_w_in, od_rpb, od_w_out, router_w, router_b,
           exp_w1, exp_b1, exp_w2, exp_b2, final_g):
    bsz, s_len, d = x.shape
    n_ctx = ctx.shape[1]
    depth = ada_w.shape[0]
    n_lat = bsz * s_len
    assert bsz < MOD_ROWS and s_len % 512 == 0 and n_lat % n_ctx == 0

    def seg_of_tile(i, tm):
        return jnp.minimum(i * tm // s_len, bsz)

    c_rows = jnp.concatenate([c, c_ctx[None, :], jnp.zeros((MOD_ROWS - bsz - 1, d), F32)], axis=0)
    mods = _ada_modulation(c_rows, ada_w, ada_b)

    t = n_lat + bsz * n_ctx
    tables = _rope_tables(s_len, _row_tile(t))
    xs = jnp.concatenate([x.reshape(n_lat, d), ctx.reshape(bsz * n_ctx, d)], axis=0)
    for layer in range(depth):
        i = layer // 2
        last = layer == depth - 1
        mod = mods[layer].reshape(MOD_ROWS * 6, 1, d)
        if layer % 2 == 0:
            a_list, w_list = _even_layer_attn(xs, mod, seg_of_tile, norm1_g[layer], ev_w_in[i], ev_sink[i],
                                              ev_q_norm_g[i], ev_w_uq[i], ev_kv_norm_g[i], ev_w_ukv[i],
                                              ev_w_out[i], tables, bsz, s_len, n_ctx)
        else:
            a_list, w_list = _odd_layer_attn(xs, mod, seg_of_tile, norm1_g[layer], od_w_in[i], od_rpb[i],
                                             od_w_out[i], bsz, s_len, n_ctx)
        xs, h2, top_idx, top_gate, rank, counts = _out_router(a_list, w_list, xs, mod, norm2_g[layer],
                                                              router_w[layer], router_b[layer], seg_of_tile)
        yk = _moe(h2, top_idx, rank, counts[:, 0].astype(jnp.int32), layer, exp_w1, exp_b1, exp_w2, exp_b2)
        xs = _combine(xs, yk, top_gate.T, mod, seg_of_tile, n_lat if last else t, final_g if last else None)
    return xs.reshape(bsz, s_len, d)
```

```python
import functools

import numpy as np
import jax
import jax.numpy as jnp
from jax import lax
from jax.experimental import pallas as pl
from jax.experimental.pallas import tpu as pltpu

GRID_W = 64
HEAD_DIM = 64
ROPE_BASE = 10000.0
NORM_EPS = 1e-6
NEG_INF = -1e30

A_HEADS = 8
A_KV_HEADS = 2
A_WINDOW = 128
B_HEADS = 8
B_NOPE = 64
B_ROPE = 32
B_V = 64
B_Q_RANK = 768
B_KV_RANK = 256
C_HEADS = 16
NA_ROWS = 8
NA_COLS = 16

N_EXPERTS = 32
TOP_K = 4
D_EXPERT = 1024
SWIGLU_LIMIT = 7.0
SWIGLU_ALPHA = 1.702

LANES = 128
MOD_ROWS = 8
VMEM_LIMIT = 48 * 1024 * 1024
MOE_VMEM_LIMIT = 56 * 1024 * 1024

BF16 = jnp.bfloat16
F32 = jnp.float32
NT_DIMS = (((1,), (1,)), ((), ()))
LOG2_E = 1.4426950408889634


def _cparams(*sem):
    return pltpu.CompilerParams(dimension_semantics=sem, vmem_limit_bytes=VMEM_LIMIT)


def _dot(a, b):
    return jnp.dot(a, b, preferred_element_type=F32)


def _dot_nt(a, b):
    return lax.dot_general(a, b, NT_DIMS, preferred_element_type=F32)


def _ada_kernel(c_ref, w_ref, b_ref, o_ref):
    c = c_ref[...]
    s = c / (1.0 + jnp.exp(-c))
    o_ref[0] = jnp.dot(s, w_ref[0], preferred_element_type=F32, precision=lax.Precision.HIGHEST) + b_ref[0]


def _ada_modulation(c_rows, ada_w, ada_b):
    depth, d, n = ada_w.shape
    tn = 1024
    return pl.pallas_call(
        _ada_kernel,
        grid=(depth, n // tn),
        in_specs=[
            pl.BlockSpec((MOD_ROWS, d), lambda l, j: (0, 0)),
            pl.BlockSpec((1, d, tn), lambda l, j: (l, 0, j)),
            pl.BlockSpec((1, 1, tn), lambda l, j: (l, 0, j)),
        ],
        out_specs=pl.BlockSpec((1, MOD_ROWS, tn), lambda l, j: (l, 0, j)),
        out_shape=jax.ShapeDtypeStruct((depth, MOD_ROWS, n), F32),
        compiler_params=_cparams("parallel", "parallel"),
        name="ada_modulation",
    )(c_rows, ada_w, ada_b.reshape(depth, 1, n))


def _norm_mm_kernel(*refs, modulate):
    if modulate:
        x_ref, g_ref, sh_ref, sc_ref, w_ref, o_ref = refs
    else:
        x_ref, g_ref, w_ref, o_ref = refs
    x = x_ref[...].astype(F32)
    ms = jnp.mean(x * x, axis=-1, keepdims=True)
    h = x * lax.rsqrt(ms + NORM_EPS) * g_ref[...]
    if modulate:
        h = h * (1.0 + sc_ref[0]) + sh_ref[0]
    o_ref[...] = _dot(h.astype(BF16), w_ref[...]).astype(o_ref.dtype)


def _row_tile(t):
    return 512 if t % 512 == 0 else 256


def _norm_matmul(x, col_block, kdim, g, w, mod=None, shift_idx=0, scale_idx=0, seg_of_tile=None):
    t = x.shape[0]
    n = w.shape[1]
    tm = _row_tile(t)
    in_specs = [pl.BlockSpec((tm, kdim), lambda i: (i, col_block)),
                pl.BlockSpec((1, kdim), lambda i: (0, 0))]
    args = [x, g.reshape(1, kdim).astype(F32)]
    if mod is not None:
        in_specs += [pl.BlockSpec((1, 1, kdim), lambda i: (seg_of_tile(i, tm) * 6 + shift_idx, 0, 0)),
                     pl.BlockSpec((1, 1, kdim), lambda i: (seg_of_tile(i, tm) * 6 + scale_idx, 0, 0))]
        args += [mod, mod]
    in_specs.append(pl.BlockSpec((kdim, n), lambda i: (0, 0)))
    args.append(w)
    return pl.pallas_call(
        functools.partial(_norm_mm_kernel, modulate=mod is not None),
        grid=(t // tm,),
        in_specs=in_specs,
        out_specs=pl.BlockSpec((tm, n), lambda i: (i, 0)),
        out_shape=jax.ShapeDtypeStruct((t, n), BF16),
        compiler_params=_cparams("parallel"),
        name="norm_matmul",
    )(*args)


EV_QA, EV_KA, EV_VA = 0, 512, 640
EV_QB, EV_KB, EV_VB, EV_OUT = 768, 1792, 2816, 3328
P_QA, P_QA_ROT, P_KA, P_KA_ROT, P_VA, P_CQ, P_CKV, P_KPE, P_KPE_ROT, P_END = (
    0, 512, 1024, 1152, 1280, 1408, 2176, 2432, 2560, 2688)
TAB_COS64, TAB_SIN64, TAB_COSQ, TAB_SINQ, TAB_COSK, TAB_SINK, TAB_END = 0, 128, 256, 384, 512, 640, 768


def _rms(x, g):
    return x * lax.rsqrt(jnp.mean(x * x, axis=-1, keepdims=True) + NORM_EPS) * g


def _even_proj_kernel(x_ref, g_ref, sh_ref, sc_ref, w1_ref, tab_ref, gq_ref, wq_ref, gkv_ref, wk_ref, o_ref):
    h = _rms(x_ref[...], g_ref[...]) * (1.0 + sc_ref[0]) + sh_ref[0]
    p = _dot(h.astype(BF16), w1_ref[...])
    cos64, sin64 = tab_ref[:, TAB_COS64:TAB_SIN64], tab_ref[:, TAB_SIN64:TAB_COSQ]
    for b in range(A_HEADS // 2 + 1):
        lo = P_QA + LANES * b if b < A_HEADS // 2 else P_KA
        rot = P_QA_ROT + LANES * b if b < A_HEADS // 2 else P_KA_ROT
        o_ref[:, EV_QA + LANES * b:EV_QA + LANES * (b + 1)] = (
            p[:, lo:lo + LANES] * cos64 + p[:, rot:rot + LANES] * sin64).astype(BF16)
    o_ref[:, EV_VA:EV_QB] = p[:, P_VA:P_CQ].astype(BF16)

    cq = _rms(p[:, P_CQ:P_CKV], gq_ref[...]).astype(BF16)
    q2 = _dot(cq, wq_ref[...])
    cosq, sinq = tab_ref[:, TAB_COSQ:TAB_SINQ], tab_ref[:, TAB_SINQ:TAB_COSK]
    n_q = B_HEADS * LANES
    for hd in range(B_HEADS):
        lo = LANES * hd
        o_ref[:, EV_QB + lo:EV_QB + lo + LANES] = (
            q2[:, lo:lo + LANES] * cosq + q2[:, n_q + lo:n_q + lo + LANES] * sinq).astype(BF16)

    ckv = _rms(p[:, P_CKV:P_KPE], gkv_ref[...]).astype(BF16)
    kpe = (p[:, P_KPE:P_KPE_ROT] * tab_ref[:, TAB_COSK:TAB_SINK]
           + p[:, P_KPE_ROT:P_END] * tab_ref[:, TAB_SINK:TAB_END]).astype(BF16)
    o_ref[:, EV_KB:EV_OUT] = _dot(jnp.concatenate([ckv, kpe], axis=1), wk_ref[...]).astype(BF16)


def _rot_cols(w, half):
    g = w.reshape(w.shape[:-1] + (w.shape[-1] // (2 * half), 2, half))
    return jnp.stack([-g[..., 1, :], g[..., 0, :]], axis=-2).reshape(w.shape)


def _rope_tables(s_len, tm):
    t = jnp.arange(s_len, dtype=jnp.int32)
    pos = ((t // GRID_W).astype(F32)[:, None], (t % GRID_W).astype(F32)[:, None])

    def pattern(n):
        half = n // 4
        inv_freq = ROPE_BASE ** (-(jnp.arange(half, dtype=F32) / half))
        ang = jnp.concatenate([pos[0] * inv_freq, pos[0] * inv_freq, pos[1] * inv_freq, pos[1] * inv_freq], axis=1)
        return jnp.cos(ang), jnp.sin(ang)

    c64, s64 = pattern(HEAD_DIM)
    c32, s32 = pattern(B_ROPE)
    one = lambda n: jnp.ones((s_len, n), F32)
    zero = lambda n: jnp.zeros((s_len, n), F32)
    rest = LANES - B_NOPE - B_ROPE
    tab = jnp.concatenate([
        c64, c64, s64, s64,
        one(B_NOPE), c32, one(rest), zero(B_NOPE), s32, zero(rest),
        c32, one(LANES - B_ROPE), s32, zero(LANES - B_ROPE)], axis=1)
    ident = jnp.concatenate([jnp.ones((tm, LANES), F32), jnp.zeros((tm, LANES), F32)] * 3, axis=1)
    return jnp.concatenate([tab, ident], axis=0)


def _even_proj(x, mod, seg_of_tile, norm1_g, w_in, q_norm_g, w_uq, kv_norm_g, w_ukv, tables, bsz, s_len):
    t, d = x.shape
    tm = _row_tile(t)
    sizes = (A_HEADS * HEAD_DIM, A_KV_HEADS * HEAD_DIM, A_KV_HEADS * HEAD_DIM, B_Q_RANK, B_KV_RANK, B_ROPE)
    offs = np.cumsum((0,) + sizes)
    part = [w_in[:, offs[k]:offs[k + 1]] for k in range(6)]
    w_qa = part[0].reshape(d, A_HEADS, HEAD_DIM)[:, A_HEAD_ORDER, :].reshape(d, -1) * (HEAD_DIM ** -0.5 * LOG2_E)
    pad_blk = lambda w: _pad_cols(w, LANES)
    w1 = jnp.concatenate([w_qa, _rot_cols(w_qa, HEAD_DIM // 4), part[1], _rot_cols(part[1], HEAD_DIM // 4), part[2],
                          part[3], part[4], pad_blk(part[5]), pad_blk(_rot_cols(part[5], B_ROPE // 4))],
                         axis=1).astype(BF16)
    assert w1.shape[1] == P_END
    qk_dim = B_NOPE + B_ROPE
    rest = LANES - qk_dim
    wq = w_uq.reshape(B_Q_RANK, B_HEADS, qk_dim) * (qk_dim ** -0.5 * LOG2_E)
    wq_plain = jnp.pad(wq, ((0, 0), (0, 0), (0, rest)))
    wq_rot = jnp.pad(_rot_cols(wq[..., B_NOPE:], B_ROPE // 4), ((0, 0), (0, 0), (B_NOPE, rest)))
    wq2 = jnp.concatenate([wq_plain.reshape(B_Q_RANK, -1), wq_rot.reshape(B_Q_RANK, -1)], axis=1).astype(BF16)
    wkv = w_ukv.reshape(B_KV_RANK, B_HEADS, B_NOPE + B_V)
    wk = jnp.pad(wkv[..., :B_NOPE], ((0, 0), (0, 0), (0, LANES - B_NOPE))).reshape(B_KV_RANK, -1)
    place = jnp.pad(jnp.eye(B_ROPE, dtype=F32), ((0, LANES - B_ROPE), (B_NOPE, rest)))
    wk2 = jnp.concatenate([
        jnp.concatenate([wk, wkv[..., B_NOPE:].reshape(B_KV_RANK, -1)], axis=1),
        jnp.concatenate([jnp.tile(place, (1, B_HEADS)), jnp.zeros((LANES, B_HEADS * B_V), F32)], axis=1)],
        axis=0).astype(BF16)
    n_lat_tiles = bsz * s_len // tm
    per_batch = s_len // tm

    def tab_map(i):
        return (jnp.where(i < n_lat_tiles, i % per_batch, per_batch), 0)

    const = lambda i: (0, 0)
    return pl.pallas_call(
        _even_proj_kernel,
        grid=(t // tm,),
        in_specs=[
            pl.BlockSpec((tm, d), lambda i: (i, 0)),
            pl.BlockSpec((1, d), const),
            pl.BlockSpec((1, 1, d), lambda i: (seg_of_tile(i, tm) * 6 + 0, 0, 0)),
            pl.BlockSpec((1, 1, d), lambda i: (seg_of_tile(i, tm) * 6 + 1, 0, 0)),
            pl.BlockSpec(w1.shape, const),
            pl.BlockSpec((tm, TAB_END), tab_map),
            pl.BlockSpec((1, B_Q_RANK), const),
            pl.BlockSpec(wq2.shape, const),
            pl.BlockSpec((1, B_KV_RANK), const),
            pl.BlockSpec(wk2.shape, const),
        ],
        out_specs=pl.BlockSpec((tm, EV_OUT), lambda i: (i, 0)),
        out_shape=jax.ShapeDtypeStruct((t, EV_OUT), BF16),
        compiler_params=pltpu.CompilerParams(dimension_semantics=("parallel",), vmem_limit_bytes=MOE_VMEM_LIMIT),
        name="even_proj",
    )(x, norm1_g.reshape(1, d), mod, mod, w1, tables, q_norm_g.reshape(1, -1), wq2,
      kv_norm_g.reshape(1, -1), wk2)


def _out_router_kernel(*refs, n_a, n_lat_tiles):
    al_refs = refs[:n_a]
    ac_refs = refs[n_a:2 * n_a]
    w_refs = refs[2 * n_a:3 * n_a]
    x_ref, gate_ref, g2_ref, sh_ref, sc_ref, rwt_ref, rb_ref, tri_ref = refs[3 * n_a:3 * n_a + 8]
    xo_ref, h_ref, ti_ref, tg_ref, rk_ref, cnt_ref, cnt_scr = refs[3 * n_a + 8:]
    is_latent = pl.program_id(0) < n_lat_tiles
    acc = None
    for k in range(n_a):
        a = jnp.where(is_latent, al_refs[k][...], ac_refs[k][...])
        part = _dot(a, w_refs[k][...])
        acc = part if acc is None else acc + part
    xn = x_ref[...] + gate_ref[0] * acc
    xo_ref[...] = xn
    ms = jnp.mean(xn * xn, axis=-1, keepdims=True)
    h = xn * lax.rsqrt(ms + NORM_EPS) * g2_ref[...]
    h = h * (1.0 + sc_ref[0]) + sh_ref[0]
    h_ref[...] = h.astype(BF16)
    h_hi = h.astype(BF16)
    h_lo = (h - h_hi.astype(F32)).astype(BF16)
    both = _dot_nt(rwt_ref[...], h_hi)
    logits = (both[:N_EXPERTS] + both[N_EXPERTS:]) + _dot_nt(rwt_ref[:N_EXPERTS, :], h_lo) + rb_ref[...]
    eidx = lax.broadcasted_iota(jnp.int32, logits.shape, 0)
    vals, idxs, hots = [], [], []
    cur = logits
    for _ in range(TOP_K):
        m = jnp.max(cur, axis=0, keepdims=True)
        idx = jnp.min(jnp.where(cur == m, eidx, N_EXPERTS), axis=0, keepdims=True)
        hot = eidx == idx
        vals.append(m)
        idxs.append(idx)
        hots.append(hot)
        cur = jnp.where(hot, -jnp.inf, cur)
    es = [jnp.exp(v - vals[0]) for v in vals]
    den = es[0] + es[1] + es[2] + es[3]
    ti_ref[...] = jnp.concatenate(idxs, axis=0)
    tg_ref[...] = jnp.concatenate([e / den for e in es], axis=0)

    @pl.when(pl.program_id(0) == 0)
    def _():
        cnt_scr[...] = jnp.zeros_like(cnt_scr)

    chosen = jnp.where(hots[0] | hots[1] | hots[2] | hots[3], 1.0, 0.0)
    before = cnt_scr[...] + _dot(chosen.astype(BF16), tri_ref[...])
    rk_ref[...] = jnp.concatenate(
        [jnp.sum(jnp.where(hot, before, 0.0), axis=0, keepdims=True) for hot in hots], axis=0).astype(jnp.int32)
    total = cnt_scr[...] + jnp.sum(chosen, axis=1, keepdims=True)
    cnt_scr[...] = total
    cnt_ref[...] = jnp.broadcast_to(total, cnt_ref.shape)


def _out_router(a_list, w_list, x, mod, g2, router_w, router_b, seg_of_tile):
    t, d = x.shape
    tm = _row_tile(t)
    n_a = len(a_list)
    n_lat_tiles = a_list[0][0].shape[0] // tm

    def mod_spec(idx):
        return pl.BlockSpec((1, 1, d), lambda i: (seg_of_tile(i, tm) * 6 + idx, 0, 0))

    in_specs = [pl.BlockSpec((tm, al.shape[1]), lambda i: (jnp.minimum(i, n_lat_tiles - 1), 0)) for al, _ in a_list]
    in_specs += [pl.BlockSpec((tm, ac.shape[1]), lambda i: (jnp.maximum(i - n_lat_tiles, 0), 0)) for _, ac in a_list]
    in_specs += [pl.BlockSpec(w.shape, lambda i: (0, 0)) for w in w_list]
    in_specs += [pl.BlockSpec((tm, d), lambda i: (i, 0)), mod_spec(2),
                 pl.BlockSpec((1, d), lambda i: (0, 0)), mod_spec(3), mod_spec(4),
                 pl.BlockSpec((2 * N_EXPERTS, d), lambda i: (0, 0)),
                 pl.BlockSpec((N_EXPERTS, 1), lambda i: (0, 0)),
                 pl.BlockSpec((tm, tm), lambda i: (0, 0))]
    rw_hi = router_w.T.astype(BF16)
    rw_lo = (router_w.T - rw_hi.astype(F32)).astype(BF16)
    strictly_upper = jnp.asarray(np.triu(np.ones((tm, tm), np.float32), k=1), BF16)
    return pl.pallas_call(
        functools.partial(_out_router_kernel, n_a=n_a, n_lat_tiles=n_lat_tiles),
        grid=(t // tm,),
        in_specs=in_specs,
        out_specs=[pl.BlockSpec((tm, d), lambda i: (i, 0)),
                   pl.BlockSpec((tm, d), lambda i: (i, 0)),
                   pl.BlockSpec((TOP_K, tm), lambda i: (0, i)),
                   pl.BlockSpec((TOP_K, tm), lambda i: (0, i)),
                   pl.BlockSpec((TOP_K, tm), lambda i: (0, i)),
                   pl.BlockSpec((N_EXPERTS, LANES), lambda i: (0, 0))],
        out_shape=[jax.ShapeDtypeStruct((t, d), F32),
                   jax.ShapeDtypeStruct((t, d), BF16),
                   jax.ShapeDtypeStruct((TOP_K, t), jnp.int32),
                   jax.ShapeDtypeStruct((TOP_K, t), F32),
                   jax.ShapeDtypeStruct((TOP_K, t), jnp.int32),
                   jax.ShapeDtypeStruct((N_EXPERTS, LANES), F32)],
        scratch_shapes=[pltpu.VMEM((N_EXPERTS, 1), F32)],
        compiler_params=_cparams("arbitrary"),
        name="out_router",
    )(*[al for al, _ in a_list], *[ac for _, ac in a_list], *w_list, x, mod, g2.reshape(1, d), mod, mod,
      jnp.concatenate([rw_hi, rw_lo], axis=0), router_b.reshape(N_EXPERTS, 1), strictly_upper)


def _half_mask(shape, j):
    lane = lax.broadcasted_iota(jnp.int32, shape, 1)
    return (lane >= HEAD_DIM * j) & (lane < HEAD_DIM * (j + 1))


def _softmax_pv(scores, values, sink=None):
    m = jnp.max(scores[0], axis=1, keepdims=True)
    for s in scores[1:]:
        m = jnp.maximum(m, jnp.max(s, axis=1, keepdims=True))
    if sink is not None:
        m = jnp.maximum(m, sink)
    den = None
    out = None
    for s, v in zip(scores, values):
        p = jnp.exp2(s - m)
        ps = jnp.sum(p, axis=1, keepdims=True)
        den = ps if den is None else den + ps
        o = _dot(p.astype(BF16), v)
        out = o if out is None else out + o
    if sink is not None:
        den = den + jnp.exp2(sink - m)
    return out / den


WINDOW_SUB = 256


def _pair_attn_kernel(*refs, window, tq, s_len, has_sink, heads_per_group):
    if window:
        q_ref, kl_ref, vl_ref, kc_ref, vc_ref = refs[:5]
        rest = refs[5:]
    else:
        q_ref, kc_ref, vc_ref = refs[:3]
        rest = refs[3:]
    if has_sink:
        sink_ref, o_ref = rest
    else:
        (o_ref,) = rest
    blk = pl.program_id(1)
    kc = kc_ref[...]
    vc = vc_ref[...]
    sub = min(tq, WINDOW_SUB) if window else tq
    for u in range(tq // sub):
        q = q_ref[u * sub:(u + 1) * sub, :]
        if window:
            wl = sub + 2 * A_WINDOW
            q0 = pl.program_id(2) * tq + u * sub
            start = pl.multiple_of(jnp.clip(q0 - A_WINDOW, 0, s_len - wl), LANES)
            kw = kl_ref[pl.ds(start, wl), :]
            vw = vl_ref[pl.ds(start, wl), :]
            qpos = q0 + lax.broadcasted_iota(jnp.int32, (sub, wl), 0)
            kpos = start + lax.broadcasted_iota(jnp.int32, (sub, wl), 1)
            band = jnp.abs(qpos - kpos) <= A_WINDOW
        outs = []
        for j in range(2):
            qj = jnp.where(_half_mask(q.shape, j), q, jnp.zeros_like(q))
            scores, values = [], []
            if window:
                scores.append(jnp.where(band, _dot_nt(qj, kw), NEG_INF))
                values.append(vw)
            scores.append(_dot_nt(qj, kc))
            values.append(vc)
            sink = sink_ref[j * heads_per_group + blk] if has_sink else None
            outs.append(_softmax_pv(scores, values, sink))
        o_ref[u * sub:(u + 1) * sub, :] = jnp.where(_half_mask(outs[0].shape, 0), outs[0], outs[1]).astype(o_ref.dtype)


def _pair_attn(q_arr, q_cb, k_arr, k_cb, v_arr, v_cb, n_blk, bsz, s_len, n_ctx, sink, window):
    ctx_blk0 = bsz * s_len // n_ctx
    if window:
        tq = min(1024, s_len)
        nq = s_len // tq
        q_spec = pl.BlockSpec((tq, LANES), lambda b, h, i: (b * nq + i, q_cb(h)))
        kv_specs = [pl.BlockSpec((s_len, LANES), lambda b, h, i: (b, k_cb(h))),
                    pl.BlockSpec((s_len, LANES), lambda b, h, i: (b, v_cb(h)))]
        args = [q_arr, k_arr, v_arr, k_arr, v_arr]
        out_rows = bsz * s_len
    else:
        tq = n_ctx
        nq = 1
        q_spec = pl.BlockSpec((tq, LANES), lambda b, h, i: (ctx_blk0 + b, q_cb(h)))
        kv_specs = []
        args = [q_arr, k_arr, v_arr]
        out_rows = bsz * n_ctx
    kv_specs += [pl.BlockSpec((n_ctx, LANES), lambda b, h, i: (ctx_blk0 + b, k_cb(h))),
                 pl.BlockSpec((n_ctx, LANES), lambda b, h, i: (ctx_blk0 + b, v_cb(h)))]
    in_specs = [q_spec] + kv_specs
    if sink is not None:
        in_specs.append(pl.BlockSpec(memory_space=pltpu.SMEM))
        args.append(sink.astype(F32) * LOG2_E)
    return pl.pallas_call(
        functools.partial(_pair_attn_kernel, window=window, tq=tq, s_len=s_len,
                          has_sink=sink is not None, heads_per_group=n_blk),
        grid=(bsz, n_blk, nq),
        in_specs=in_specs,
        out_specs=pl.BlockSpec((tq, LANES), lambda b, h, i: (b * nq + i, h)),
        out_shape=jax.ShapeDtypeStruct((out_rows, n_blk * LANES), BF16),
        compiler_params=_cparams("parallel", "parallel", "arbitrary"),
        name="pair_attn_window" if window else "pair_attn_ctx",
    )(*args)


def _mla_kernel(*refs, latent, tk, n_chunks, sub):
    if latent:
        q_ref, kl_ref, vl_ref, kc_ref, vc_ref, o_ref = refs
    else:
        q_ref, kc_ref, vc_ref, o_ref = refs
    streams = [(j, r) for j in range(2) for r in range(q_ref.shape[0] // sub)]

    def update(j, r, k, v, state):
        m, l, acc = state
        s = _dot_nt(q_ref[r * sub:(r + 1) * sub, LANES * j:LANES * (j + 1)], k)
        m_new = jnp.maximum(m, jnp.max(s, axis=1, keepdims=True))
        p = jnp.exp2(s - m_new)
        alpha = jnp.exp2(m - m_new)
        l_new = alpha * l + jnp.sum(p, axis=1, keepdims=True)
        acc_new = alpha * acc + _dot(p.astype(BF16), v)
        return m_new, l_new, acc_new

    init = (jnp.full((sub, 1), NEG_INF, F32), jnp.zeros((sub, 1), F32), jnp.zeros((sub, LANES), F32))
    vc = vc_ref[...]
    state = tuple(update(j, r, kc_ref[:, LANES * j:LANES * (j + 1)], vc, init) for j, r in streams)
    if latent:
        for c in range(n_chunks):
            v = vl_ref[c * tk:(c + 1) * tk, :]
            state = tuple(update(j, r, kl_ref[c * tk:(c + 1) * tk, LANES * j:LANES * (j + 1)], v, st)
                          for (j, r), st in zip(streams, state))
    for r in range(q_ref.shape[0] // sub):
        o0, o1 = [state[streams.index((j, r))] for j in range(2)]
        o0 = o0[2] / o0[1]
        o1 = o1[2] / o1[1]
        o_ref[r * sub:(r + 1) * sub, :] = jnp.where(_half_mask(o0.shape, 0), o0, o1).astype(o_ref.dtype)


def _mla_attn(arr, q_col, k_col, v_col, bsz, s_len, n_ctx, latent):
    n_pair = B_HEADS // 2
    ctx_blk0 = bsz * s_len // n_ctx
    qc, kc, vc = q_col // (2 * LANES), k_col // (2 * LANES), v_col // LANES
    kv_specs = [pl.BlockSpec((n_ctx, 2 * LANES), lambda b, h, i: (ctx_blk0 + b, kc + h)),
                pl.BlockSpec((n_ctx, LANES), lambda b, h, i: (ctx_blk0 + b, vc + h))]
    if latent:
        tq, tk = 512, min(2048, s_len)
        nq = s_len // tq
        q_spec = pl.BlockSpec((tq, 2 * LANES), lambda b, h, i: (b * nq + i, qc + h))
        kv_specs = [pl.BlockSpec((s_len, 2 * LANES), lambda b, h, i: (b, kc + h)),
                    pl.BlockSpec((s_len, LANES), lambda b, h, i: (b, vc + h))] + kv_specs
        args = [arr] * 5
        out_rows = bsz * s_len
    else:
        tq, tk = n_ctx, n_ctx
        nq = 1
        q_spec = pl.BlockSpec((tq, 2 * LANES), lambda b, h, i: (ctx_blk0 + b, qc + h))
        args = [arr] * 3
        out_rows = bsz * n_ctx
    return pl.pallas_call(
        functools.partial(_mla_kernel, latent=latent, tk=tk, n_chunks=s_len // tk, sub=min(tq, 512)),
        grid=(bsz, n_pair, nq),
        in_specs=[q_spec] + kv_specs,
        out_specs=pl.BlockSpec((tq, LANES), lambda b, h, i: (b * nq + i, h)),
        out_shape=jax.ShapeDtypeStruct((out_rows, n_pair * LANES), BF16),
        compiler_params=_cparams("parallel", "parallel", "arbitrary"),
        name="mla_latent" if latent else "mla_ctx",
    )(*args)


NA_Q_ROWS = 4
NA_K_ROWS = NA_Q_ROWS + NA_ROWS


NA_SUB = 4


def _na_kernel(pat_ref, start_ref, q_ref, kl_ref, vl_ref, kc_ref, vc_ref, *rest):
    del pat_ref
    bias_refs, o_ref = rest[:-1], rest[-1]
    rb = pl.program_id(2)
    nk = NA_K_ROWS * GRID_W
    tq = NA_Q_ROWS * GRID_W
    kc = kc_ref[...]
    vc = vc_ref[...]
    for u, bias_ref in enumerate(bias_refs):
        start = pl.multiple_of(start_ref[rb * len(bias_refs) + u] * GRID_W, NA_Q_ROWS * GRID_W)
        kw = kl_ref[pl.ds(start, nk), :]
        vw = vl_ref[pl.ds(start, nk), :]
        q = q_ref[u * tq:(u + 1) * tq, :]
        outs = []
        for j in range(2):
            qj = jnp.where(_half_mask(q.shape, j), q, jnp.zeros_like(q))
            sw = _dot_nt(qj, kw) + bias_ref[0, j]
            sc = _dot_nt(qj, kc)
            outs.append(_softmax_pv([sw, sc], [vw, vc]))
        o_ref[u * tq:(u + 1) * tq, :] = jnp.where(_half_mask(outs[0].shape, 0), outs[0], outs[1]).astype(o_ref.dtype)


def _na_patterns(n_rows):
    kh = NA_ROWS
    n_rb = n_rows // NA_Q_ROWS
    starts, keys = [], []
    for rb in range(n_rb):
        r_a = rb * NA_Q_ROWS
        start = int(np.clip(r_a - NA_Q_ROWS, 0, n_rows - NA_K_ROWS))
        assert start % NA_Q_ROWS == 0
        rows = r_a + np.arange(NA_Q_ROWS)
        r0 = np.clip(rows - kh // 2, 0, n_rows - kh)
        assert start <= r0.min() and r0.max() + kh <= start + NA_K_ROWS
        starts.append(start)
        keys.append((r_a - start, tuple((r0 - start).tolist())))
    uniq = sorted(set(keys))
    pat = [uniq.index(k) for k in keys]
    return np.asarray(starts, np.int32), np.asarray(pat, np.int32), uniq


def _na_bias_table(rpb, uniq):
    n_dc = 2 * NA_COLS - 1
    i = np.arange(NA_Q_ROWS)[:, None]
    j = np.arange(NA_K_ROWS)[None, :]
    c = np.arange(GRID_W)[:, None]
    kc = np.arange(GRID_W)[None, :]
    c0 = np.clip(c - NA_COLS // 2, 0, GRID_W - NA_COLS)
    cvalid = (kc >= c0) & (kc < c0 + NA_COLS)
    dc = np.clip(kc - c + NA_COLS - 1, 0, n_dc - 1)
    csel = np.eye(n_dc, dtype=np.float32)[dc] * cvalid[..., None]
    tiles = jnp.einsum('hab,cqb->hacq', rpb.astype(F32), jnp.asarray(csel), precision=lax.Precision.HIGHEST)
    tiles = jnp.where(jnp.asarray(cvalid), tiles * LOG2_E, NEG_INF)
    masked = jnp.full((C_HEADS, GRID_W, GRID_W), NEG_INF, F32)
    tables = []
    for delta, r0_rel in uniq:
        r0_rel = np.asarray(r0_rel)[:, None]
        valid = (j >= r0_rel) & (j < r0_rel + NA_ROWS)
        dr = j - (delta + i) + NA_ROWS - 1
        rows = [jnp.concatenate([tiles[:, dr[qi, kj]] if valid[qi, kj] else masked for kj in range(NA_K_ROWS)],
                                axis=-1) for qi in range(NA_Q_ROWS)]
        tables.append(jnp.concatenate(rows, axis=-2))
    return jnp.stack(tables)


def _na_attn(qkv, rpb, bsz, s_len, n_ctx):
    n_pair = C_HEADS // 2
    n_rows = s_len // GRID_W
    starts, pat, uniq = _na_patterns(n_rows)
    bias = _na_bias_table(rpb, uniq)
    n_sub = min(NA_SUB, n_rows // NA_Q_ROWS)
    n_rb = n_rows // NA_Q_ROWS // n_sub
    tq = NA_Q_ROWS * GRID_W * n_sub
    nk = NA_K_ROWS * GRID_W
    ctx_blk0 = bsz * s_len // n_ctx
    bias_specs = [pl.BlockSpec((1, 2, tq // n_sub, nk), lambda h, b, r, pat, st, u=u: (pat[r * n_sub + u], h, 0, 0))
                  for u in range(n_sub)]
    grid_spec = pltpu.PrefetchScalarGridSpec(
        num_scalar_prefetch=2,
        grid=(n_pair, bsz, n_rb),
        in_specs=[
            pl.BlockSpec((tq, LANES), lambda h, b, r, pat, st: (b * n_rb + r, h)),
            pl.BlockSpec((s_len, LANES), lambda h, b, r, pat, st: (b, n_pair + h)),
            pl.BlockSpec((s_len, LANES), lambda h, b, r, pat, st: (b, 2 * n_pair + h)),
            pl.BlockSpec((n_ctx, LANES), lambda h, b, r, pat, st: (ctx_blk0 + b, n_pair + h)),
            pl.BlockSpec((n_ctx, LANES), lambda h, b, r, pat, st: (ctx_blk0 + b, 2 * n_pair + h)),
        ] + bias_specs,
        out_specs=pl.BlockSpec((tq, LANES), lambda h, b, r, pat, st: (b * n_rb + r, h)),
    )
    return pl.pallas_call(
        _na_kernel,
        grid_spec=grid_spec,
        out_shape=jax.ShapeDtypeStruct((bsz * s_len, n_pair * LANES), BF16),
        compiler_params=_cparams("parallel", "parallel", "arbitrary"),
        name="na_latent",
    )(jnp.asarray(pat), jnp.asarray(starts), qkv, qkv, qkv, qkv, qkv, *([bias] * n_sub))


MOE_BLOCK = 512
MOE_FC = 512
MOE_PARTS = 4


def _moe_kernel(*refs, layer):
    be_ref, nu_ref, first_ref, slot_ref, next_ref, x_ref, b1_ref, b2_ref, w1_hbm, w2_hbm = refs[:10]
    o_ref, w1_f32, w2_f32, w1_bf, w2_bf, sem = refs[-6:]
    i = pl.program_id(0)

    def weight_copies(expert, slot):
        return (pltpu.make_async_copy(w1_hbm.at[layer, expert], w1_f32.at[slot], sem.at[0, slot]),
                pltpu.make_async_copy(w2_hbm.at[layer, expert], w2_f32.at[slot], sem.at[1, slot]))

    @pl.when(i < nu_ref[0])
    def _():
        @pl.when(first_ref[i] == 1)
        def _():
            slot = slot_ref[i]

            @pl.when(i == 0)
            def _():
                for cp in weight_copies(be_ref[i], slot):
                    cp.start()

            for cp in weight_copies(be_ref[i], slot):
                cp.wait()

            @pl.when(next_ref[i] >= 0)
            def _():
                for cp in weight_copies(next_ref[i], 1 - slot):
                    cp.start()

            w1_bf[...] = w1_f32[slot].astype(BF16)
            w2_bf[...] = w2_f32[slot].astype(BF16)

        x = x_ref[...]
        acc = None
        for c in range(D_EXPERT // MOE_FC):
            lo, hi = c * MOE_FC, (c + 1) * MOE_FC
            glu = _dot(x, w1_bf[:, lo:hi]) + b1_ref[0, 0, :, lo:hi]
            lin = _dot(x, w1_bf[:, D_EXPERT + lo:D_EXPERT + hi]) + b1_ref[0, 0, :, D_EXPERT + lo:D_EXPERT + hi]
            glu = jnp.minimum(glu, SWIGLU_LIMIT)
            lin = jnp.clip(lin, -SWIGLU_LIMIT, SWIGLU_LIMIT)
            act = glu * (1.0 / (1.0 + jnp.exp(-SWIGLU_ALPHA * glu))) * (lin + 1.0)
            y = _dot(act.astype(BF16), w2_bf[lo:hi, :])
            acc = y if acc is None else acc + y
        o_ref[...] = (acc + b2_ref[0, 0]).astype(o_ref.dtype)

    @pl.when((i == 0) & (nu_ref[0] <= 0))
    def _():
        o_ref[...] = jnp.zeros_like(o_ref)


def _moe_experts(xs, blk_exp, n_used, layer, w1, b1, w2, b2, n_slot, blk0, prev=None):
    d = xs.shape[1]
    n_blk = xs.shape[0] // MOE_BLOCK

    idx = jnp.arange(n_blk, dtype=jnp.int32)
    prev_exp = jnp.concatenate([jnp.full((1,), -1, jnp.int32), blk_exp[:-1]])
    first = (((idx == 0) | (blk_exp != prev_exp)) & (idx < n_used[0])).astype(jnp.int32)
    slot = (jnp.cumsum(first) - 1) % 2
    starts = jnp.where(first == 1, idx, n_blk)
    next_start = lax.cummin(jnp.concatenate([starts[1:], jnp.full((1,), n_blk, jnp.int32)]), reverse=True)
    next_exp = jnp.where(next_start < n_blk, blk_exp[jnp.minimum(next_start, n_blk - 1)], -1)

    def blk(i, nu):
        return jnp.maximum(jnp.minimum(i, nu[0] - 1), 0)

    def bias_map(i, be, nu, *_):
        return (layer, be[blk(i, nu)], 0, 0)

    in_specs = [
        pl.BlockSpec((MOE_BLOCK, d), lambda i, be, nu, *_: (blk(i, nu), 0)),
        pl.BlockSpec((1, 1, 1, 2 * D_EXPERT), bias_map),
        pl.BlockSpec((1, 1, 1, d), bias_map),
        pl.BlockSpec(memory_space=pl.ANY),
        pl.BlockSpec(memory_space=pl.ANY),
    ]
    depth = w1.shape[0]
    args = [blk_exp, n_used, first, slot.astype(jnp.int32), next_exp.astype(jnp.int32), xs,
            b1.reshape(depth, N_EXPERTS, 1, 2 * D_EXPERT), b2.reshape(depth, N_EXPERTS, 1, d), w1, w2]
    aliases = {}
    if prev is not None:
        in_specs.append(pl.BlockSpec(memory_space=pl.ANY))
        aliases = {len(args): 0}
        args.append(prev)
    grid_spec = pltpu.PrefetchScalarGridSpec(
        num_scalar_prefetch=5,
        grid=(n_blk,),
        in_specs=in_specs,
        out_specs=pl.BlockSpec((MOE_BLOCK, d), lambda i, be, nu, *_: (blk0 + blk(i, nu), 0)),
        scratch_shapes=[pltpu.VMEM((2, d, 2 * D_EXPERT), F32), pltpu.VMEM((2, D_EXPERT, d), F32),
                        pltpu.VMEM((d, 2 * D_EXPERT), BF16), pltpu.VMEM((D_EXPERT, d), BF16),
                        pltpu.SemaphoreType.DMA((2, 2))],
    )
    return pl.pallas_call(
        functools.partial(_moe_kernel, layer=layer),
        grid_spec=grid_spec,
        out_shape=jax.ShapeDtypeStruct((n_slot, d), BF16),
        input_output_aliases=aliases,
        compiler_params=pltpu.CompilerParams(dimension_semantics=("arbitrary",),
                                             vmem_limit_bytes=MOE_VMEM_LIMIT),
        name="moe_experts",
    )(*args)


def _combine_kernel(*refs, final):
    if final:
        x_ref, y_ref, g_ref, gate_ref, gf_ref, o_ref = refs
    else:
        x_ref, y_ref, g_ref, gate_ref, o_ref = refs
    g = g_ref[...]
    acc = y_ref[0].astype(F32) * g[:, 0:1]
    for k in range(1, TOP_K):
        acc = acc + y_ref[k].astype(F32) * g[:, k:k + 1]
    xn = x_ref[...] + gate_ref[0] * acc
    o_ref[...] = _rms(xn, gf_ref[...]) if final else xn


def _combine(x, yk, gates, mod, seg_of_tile, rows, final_g=None):
    d = x.shape[1]
    tm = _row_tile(rows)
    final = final_g is not None
    in_specs = [pl.BlockSpec((tm, d), lambda i: (i, 0)),
                pl.BlockSpec((TOP_K, tm, d), lambda i: (0, i, 0)),
                pl.BlockSpec((tm, TOP_K), lambda i: (i, 0)),
                pl.BlockSpec((1, 1, d), lambda i: (seg_of_tile(i, tm) * 6 + 5, 0, 0))]
    args = [x, yk, gates, mod]
    if final:
        in_specs.append(pl.BlockSpec((1, d), lambda i: (0, 0)))
        args.append(final_g.reshape(1, d))
    return pl.pallas_call(
        functools.partial(_combine_kernel, final=final),
        grid=(rows // tm,),
        in_specs=in_specs,
        out_specs=pl.BlockSpec((tm, d), lambda i: (i, 0)),
        out_shape=jax.ShapeDtypeStruct((rows, d), F32),
        compiler_params=_cparams("parallel"),
        name="moe_combine",
    )(*args)


def _moe(h, top_idx, rank, counts, layer, w1, b1, w2, b2):
    t, d = h.shape
    n_asg = t * TOP_K
    padded = (counts + MOE_BLOCK - 1) // MOE_BLOCK * MOE_BLOCK
    pad_end = jnp.cumsum(padded)
    pad_start = pad_end - padded
    grp_start = jnp.cumsum(counts) - counts
    experts = jnp.arange(N_EXPERTS, dtype=jnp.int32)
    dest = rank + jnp.sum(jnp.where(top_idx[:, :, None] == experts, pad_start, 0), axis=-1)
    dest_flat = dest.reshape(n_asg)
    assert t <= 1 << 16
    keys = (top_idx << 16) + jnp.arange(t, dtype=jnp.int32)[None, :]
    tok_sorted = lax.sort(keys.reshape(n_asg)) & 0xFFFF
    n_blk = (n_asg + N_EXPERTS * (MOE_BLOCK - 1) + MOE_BLOCK - 1) // MOE_BLOCK
    n_blk = -(-n_blk // MOE_PARTS) * MOE_PARTS
    blk_start = jnp.arange(n_blk, dtype=jnp.int32) * MOE_BLOCK
    blk_exp = jnp.minimum(jnp.sum((pad_end[None, :] <= blk_start[:, None]).astype(jnp.int32), axis=1),
                          N_EXPERTS - 1)
    n_used = (pad_end[-1:] // MOE_BLOCK).astype(jnp.int32)
    shift = jnp.repeat((grp_start - pad_start)[blk_exp], MOE_BLOCK)
    src = jnp.clip(jnp.arange(n_blk * MOE_BLOCK, dtype=jnp.int32) + shift, 0, n_asg - 1)
    buf_tok = tok_sorted[src]
    per = n_blk // MOE_PARTS
    ys = None
    for part in range(MOE_PARTS):
        lo = part * per
        xs = h[buf_tok[lo * MOE_BLOCK:(lo + per) * MOE_BLOCK]]
        ys = _moe_experts(xs, blk_exp[lo:lo + per], jnp.clip(n_used - lo, 0, per), layer, w1, b1, w2, b2,
                          n_blk * MOE_BLOCK, lo, ys)
    return ys[dest_flat].reshape(TOP_K, t, d)


def _pad_cols(w, n):
    return jnp.pad(w, ((0, 0), (0, n - w.shape[1])))


A_HEAD_ORDER = tuple(h for blk in range(A_HEADS // 2) for h in (blk, blk + A_HEADS // 2))


def _even_layer_attn(x, mod, seg_of_tile, norm1_g, w_in, sink, q_norm_g, w_uq, kv_norm_g, w_ukv, w_out,
                     tables, bsz, s_len, n_ctx):
    d = x.shape[1]
    proj = _even_proj(x, mod, seg_of_tile, norm1_g, w_in, q_norm_g, w_uq, kv_norm_g, w_ukv, tables, bsz, s_len)
    n_blk = A_HEADS // 2
    q_cb = lambda h: EV_QA // LANES + h
    k_cb = lambda h: EV_KA // LANES
    v_cb = lambda h: EV_VA // LANES
    oa_l = _pair_attn(proj, q_cb, proj, k_cb, proj, v_cb, n_blk, bsz, s_len, n_ctx, sink, window=True)
    oa_c = _pair_attn(proj, q_cb, proj, k_cb, proj, v_cb, n_blk, bsz, s_len, n_ctx, sink, window=False)
    ob_l = _mla_attn(proj, EV_QB, EV_KB, EV_VB, bsz, s_len, n_ctx, latent=True)
    ob_c = _mla_attn(proj, EV_QB, EV_KB, EV_VB, bsz, s_len, n_ctx, latent=False)
    n_a = A_HEADS * HEAD_DIM
    w_oa = w_out[:n_a].reshape(A_HEADS, HEAD_DIM, d)[A_HEAD_ORDER, :, :].reshape(n_a, d)
    return [(oa_l, oa_c), (ob_l, ob_c)], [w_oa.astype(BF16), w_out[n_a:].astype(BF16)]


def _odd_layer_attn(x, mod, seg_of_tile, norm1_g, w_in, rpb, w_out, bsz, s_len, n_ctx):
    d = x.shape[1]
    width = C_HEADS * HEAD_DIM
    w1 = jnp.concatenate([w_in[:, :width] * (HEAD_DIM ** -0.5 * LOG2_E), w_in[:, width:]], axis=1)
    qkv = _norm_matmul(x, 0, d, norm1_g, w1.astype(BF16), mod, 0, 1, seg_of_tile)
    n_pair = C_HEADS // 2
    o_l = _na_attn(qkv, rpb, bsz, s_len, n_ctx)
    o_c = _pair_attn(qkv, lambda h: h, qkv, lambda h: n_pair + h, qkv, lambda h: 2 * n_pair + h,
                     n_pair, bsz, s_len, n_ctx, None, window=False)
    return [(o_l, o_c)], [w_out.astype(BF16)]


def kernel(x, c, ctx, c_ctx, ada_w, ada_b, norm1_g, norm2_g, ev_w_in, ev_sink, ev_q_norm_g, ev_w_uq,
           ev_kv_norm_g, ev_w_ukv, ev_w_out, od_w_in, od_rpb, od_w_out, router_w, router_b,
           exp_w1, exp_b1, exp_w2, exp_b2, final_g):
    bsz, s_len, d = x.shape
    n_ctx = ctx.shape[1]
    depth = ada_w.shape[0]
    n_lat = bsz * s_len
    assert bsz < MOD_ROWS and s_len % 512 == 0 and n_lat % n_ctx == 0

    def seg_of_tile(i, tm):
        return jnp.minimum(i * tm // s_len, bsz)

    c_rows = jnp.concatenate([c, c_ctx[None, :], jnp.zeros((MOD_ROWS - bsz - 1, d), F32)], axis=0)
    mods = _ada_modulation(c_rows, ada_w, ada_b)

    t = n_lat + bsz * n_ctx
    tables = _rope_tables(s_len, _row_tile(t))
    xs = jnp.concatenate([x.reshape(n_lat, d), ctx.reshape(bsz * n_ctx, d)], axis=0)
    for layer in range(depth):
        i = layer // 2
        last = layer == depth - 1
        mod = mods[layer].reshape(MOD_ROWS * 6, 1, d)
        if layer % 2 == 0:
            a_list, w_list = _even_layer_attn(xs, mod, seg_of_tile, norm1_g[layer], ev_w_in[i], ev_sink[i],
                                              ev_q_norm_g[i], ev_w_uq[i], ev_kv_norm_g[i], ev_w_ukv[i],
                                              ev_w_out[i], tables, bsz, s_len, n_ctx)
        else:
            a_list, w_list = _odd_layer_attn(xs, mod, seg_of_tile, norm1_g[layer], od_w_in[i], od_rpb[i],
                                             od_w_out[i], bsz, s_len, n_ctx)
        xs, h2, top_idx, top_gate, rank, counts = _out_router(a_list, w_list, xs, mod, norm2_g[layer],
                                                              router_w[layer], router_b[layer], seg_of_tile)
        yk = _moe(h2, top_idx, rank, counts[:, 0].astype(jnp.int32), layer, exp_w1, exp_b1, exp_w2, exp_b2)
        xs = _combine(xs, yk, top_gate.T, mod, seg_of_tile, n_lat if last else t, final_g if last else None)
    return xs.reshape(bsz, s_len, d)
```
